```python
import math
import jax, jax.numpy as jnp
from jax import lax
import numpy as np

D_MODEL = 1024
BATCH = 8
SEQ = 8192
DEPTH = 4

CTX_LEN = 256
GRID_W = 64
D_HYENA = D_MODEL // 2
D_RNN = D_MODEL // 2
D_MIX = D_HYENA + D_RNN
RNN_HEADS = 8
RNN_BLOCK = D_RNN // RNN_HEADS
D_IN = 4 * D_HYENA + 2 * D_RNN
HYENA_SHORT = 3
RNN_CONV = 4
FILTER_EMB = 33
FILTER_BANDS = (FILTER_EMB - 1) // 2
FILTER_HIDDEN = 64
FILTER_TARGET = 1e-2
FAST_DECAY = 0.3
SLOW_DECAY = 1.5
RG_C = 8.0
EPS = 1e-6

kernel_name = "hyena_rglru_hybrid_dit_block"


def rms_norm(x, g):
    xf = x.astype(jnp.float32)
    y = xf * lax.rsqrt(jnp.mean(xf * xf, axis=-1, keepdims=True) + EPS)
    return (y * g.astype(jnp.float32)).astype(x.dtype)


def grid_sincos(rows, dim):
    r, col = jnp.meshgrid(jnp.arange(rows, dtype=jnp.float32),
                          jnp.arange(GRID_W, dtype=jnp.float32), indexing='ij')
    quarter = dim // 4
    omega = 1.0 / (10000.0 ** (jnp.arange(quarter, dtype=jnp.float32) / quarter))

    def emb(pos):
        ang = pos.reshape(-1, 1) * omega[None, :]
        return jnp.concatenate([jnp.sin(ang), jnp.cos(ang)], axis=-1)

    return jnp.concatenate([emb(r), emb(col)], axis=-1)


def dwconv(u, w, pad):
    return lax.conv_general_dilated(u, w[:, None, :], window_strides=(1,), padding=[pad],
                                    dimension_numbers=('NWC', 'WIO', 'NWC'),
                                    feature_group_count=u.shape[-1])


def hyena_filter(L, f_w1, f_b1, f_w2, f_b2, f_w3, f_b3, f_w4, f_freq):
    t = jnp.linspace(0.0, 1.0, L, dtype=jnp.float32)[:, None]
    w = 2.0 * math.pi * jnp.arange(L, dtype=jnp.float32)[:, None] / L
    f = jnp.linspace(1e-4, FILTER_BANDS - 1, FILTER_BANDS, dtype=jnp.float32)[None, :]
    z = jnp.concatenate([t, jnp.cos(f * w), -jnp.sin(f * w)], axis=-1)
    fr = f_freq.astype(jnp.float32)
    h = jnp.sin(fr[0] * (z @ f_w1.astype(jnp.float32) + f_b1.astype(jnp.float32)))
    h = jnp.sin(fr[1] * (h @ f_w2.astype(jnp.float32) + f_b2.astype(jnp.float32)))
    h = jnp.sin(fr[2] * (h @ f_w3.astype(jnp.float32) + f_b3.astype(jnp.float32)))
    h = h @ f_w4.astype(jnp.float32)
    max_decay = math.log(FILTER_TARGET) / FAST_DECAY
    min_decay = math.log(FILTER_TARGET) / SLOW_DECAY
    deltas = jnp.abs(jnp.linspace(min_decay, max_decay, D_HYENA, dtype=jnp.float32))
    deltas = jnp.concatenate([deltas, deltas])
    h = h * jnp.exp(-t * deltas[None, :])
    k_f, k_b = h[:, :D_HYENA], h[:, D_HYENA:]
    l1 = jnp.sum(jnp.abs(k_f), axis=0) + jnp.sum(jnp.abs(k_b[1:]), axis=0)
    k_f = k_f / l1
    k_b = k_b / l1
    return jnp.concatenate([k_f, jnp.zeros((1, D_HYENA), jnp.float32), k_b[1:][::-1]], axis=0)


def long_conv(u, kernel, bias):
    L = u.shape[1]
    uf = jnp.fft.rfft(u.astype(jnp.float32), n=2 * L, axis=1)
    kf = jnp.fft.rfft(kernel, n=2 * L, axis=0)
    y = jnp.fft.irfft(uf * kf[None], n=2 * L, axis=1)[:, :L]
    return (y + u.astype(jnp.float32) * bias.astype(jnp.float32)).astype(u.dtype)


def rglru_coeffs(xc, wa, ba, wx, bx, lam):
    B, L, _ = xc.shape
    xh = xc.reshape(B, L, RNN_HEADS, RNN_BLOCK)
    r = jax.nn.sigmoid((jnp.einsum('blhi,hij->blhj', xh, wa).reshape(B, L, D_RNN) + ba).astype(jnp.float32))
    i = jax.nn.sigmoid((jnp.einsum('blhi,hij->blhj', xh, wx).reshape(B, L, D_RNN) + bx).astype(jnp.float32))
    log_a = -RG_C * r * jax.nn.softplus(-lam.astype(jnp.float32))
    a = jnp.exp(log_a)
    b = jnp.sqrt(-jnp.expm1(2.0 * log_a)) * (i * xc.astype(jnp.float32))
    return a, b


def linear_scan(a, b, h0, reverse):
    idx = -1 if reverse else 0
    b = b.at[:, idx].add(a[:, idx] * h0)

    def combine(e1, e2):
        a1, b1 = e1
        a2, b2 = e2
        return a1 * a2, a2 * b1 + b2

    _, h = lax.associative_scan(combine, (a, b), reverse=reverse, axis=1)
    return h


def rglru_bidir(xr, conv_w, conv_b, wa, ba, wx, bx, lam, h0_f, h0_b):
    xc = dwconv(xr, conv_w, (2, 1)) + conv_b
    a_f, b_f = rglru_coeffs(xc, wa[0], ba[0], wx[0], bx[0], lam[0])
    a_b, b_b = rglru_coeffs(xc, wa[1], ba[1], wx[1], bx[1], lam[1])
    hf = linear_scan(a_f, b_f, h0_f, False)
    hb = linear_scan(a_b, b_b, h0_b, True)
    return hf, hb


def mixer_out(p, y_rnn, kernel, hy_conv_w, hy_conv_b, hy_bias, gn_h, gn_r, w_out):
    u = dwconv(p[..., :3 * D_HYENA], hy_conv_w, (1, 1)) + hy_conv_b
    x0, x1, v = jnp.split(u, 3, axis=-1)
    y_h = long_conv(v * x1, kernel, hy_bias) * x0
    z_h = p[..., 3 * D_HYENA:4 * D_HYENA]
    z_r = p[..., 4 * D_HYENA + D_RNN:]
    y = jnp.concatenate([rms_norm(y_h, gn_h) * jax.nn.silu(z_h),
                         rms_norm(y_rnn.astype(p.dtype), gn_r) * jax.nn.silu(z_r)], axis=-1)
    return y @ w_out


def setup_inputs(seed: int = 0) -> dict:
    key = jax.random.key(seed)
    ks = jax.random.split(key, 32)
    nrm = lambda k, s: jax.random.normal(k, s, jnp.float32)
    u = jax.random.uniform(ks[27], (DEPTH, 2, D_RNN), jnp.float32, 0.81, 0.998)
    a0 = jnp.sqrt(u)
    rg_lam = -jnp.log(jnp.expm1(-jnp.log(a0) / RG_C))
    return {
        "x": nrm(ks[0], (BATCH, SEQ, D_MODEL)),
        "c": nrm(ks[1], (BATCH, D_MODEL)),
        "ctx": nrm(ks[2], (BATCH, CTX_LEN, D_MODEL)),
        "c_ctx": nrm(ks[3], (D_MODEL,)),
        "w_mod": nrm(ks[4], (DEPTH, D_MODEL, 3 * D_MODEL)) * 0.5 * D_MODEL ** -0.5,
        "b_mod": nrm(ks[5], (DEPTH, 3 * D_MODEL)) * 0.01,
        "norm_g": 1.0 + 0.05 * nrm(ks[6], (DEPTH, D_MODEL)),
        "w_in": nrm(ks[7], (DEPTH, D_MODEL, D_IN)) * D_MODEL ** -0.5,
        "hy_conv_w": nrm(ks[8], (DEPTH, HYENA_SHORT, 3 * D_HYENA)) * HYENA_SHORT ** -0.5,
        "hy_conv_b": nrm(ks[9], (DEPTH, 3 * D_HYENA)) * 0.01,
        "f_w1": nrm(ks[10], (DEPTH, FILTER_EMB, FILTER_HIDDEN)) * FILTER_EMB ** -0.5,
        "f_b1": nrm(ks[11], (DEPTH, FILTER_HIDDEN)) * 0.1,
        "f_w2": nrm(ks[12], (DEPTH, FILTER_HIDDEN, FILTER_HIDDEN)) * FILTER_HIDDEN ** -0.5,
        "f_b2": nrm(ks[13], (DEPTH, FILTER_HIDDEN)) * 0.1,
        "f_w3": nrm(ks[14], (DEPTH, FILTER_HIDDEN, FILTER_HIDDEN)) * FILTER_HIDDEN ** -0.5,
        "f_b3": nrm(ks[15], (DEPTH, FILTER_HIDDEN)) * 0.1,
        "f_w4": nrm(ks[16], (DEPTH, FILTER_HIDDEN, 2 * D_HYENA)) * FILTER_HIDDEN ** -0.5,
        "f_freq": 1.0 + 0.1 * nrm(ks[17], (DEPTH, 3, FILTER_HIDDEN)),
        "hy_bias": nrm(ks[18], (DEPTH, D_HYENA)) * 0.1,
        "rg_conv_w": nrm(ks[19], (DEPTH, RNN_CONV, D_RNN)) * RNN_CONV ** -0.5,
        "rg_conv_b": nrm(ks[20], (DEPTH, D_RNN)) * 0.01,
        "rg_wa": nrm(ks[21], (DEPTH, 2, RNN_HEADS, RNN_BLOCK, RNN_BLOCK)) * RNN_BLOCK ** -0.5,
        "rg_ba": nrm(ks[22], (DEPTH, 2, D_RNN)) * 0.01,
        "rg_wx": nrm(ks[23], (DEPTH, 2, RNN_HEADS, RNN_BLOCK, RNN_BLOCK)) * RNN_BLOCK ** -0.5,
        "rg_bx": nrm(ks[24], (DEPTH, 2, D_RNN)) * 0.01,
        "rg_lam": rg_lam,
        "br_norm_h": 1.0 + 0.05 * nrm(ks[25], (DEPTH, D_HYENA)),
        "br_norm_r": 1.0 + 0.05 * nrm(ks[26], (DEPTH, D_RNN)),
        "w_out": nrm(ks[28], (DEPTH, D_MIX, D_MODEL)) * D_MIX ** -0.5,
        "final_g": 1.0 + 0.05 * nrm(ks[29], (D_MODEL,)),
    }


def reference(x, c, ctx, c_ctx, w_mod, b_mod, norm_g, w_in, hy_conv_w, hy_conv_b,
              f_w1, f_b1, f_w2, f_b2, f_w3, f_b3, f_w4, f_freq, hy_bias,
              rg_conv_w, rg_conv_b, rg_wa, rg_ba, rg_wx, rg_bx, rg_lam,
              br_norm_h, br_norm_r, w_out, final_g):
    B, N, D = x.shape
    L_ctx = ctx.shape[1]
    rows = N // GRID_W
    xl = x + grid_sincos(rows, D).astype(x.dtype)[None]
    xc = ctx
    s_lat = jax.nn.silu(c)
    s_ctx = jax.nn.silu(c_ctx)
    r0, r1 = 4 * D_HYENA, 4 * D_HYENA + D_RNN
    for l in range(DEPTH):
        last = l == DEPTH - 1
        mod_l = s_lat @ w_mod[l] + b_mod[l]
        mod_c = s_ctx @ w_mod[l] + b_mod[l]
        sh_l, sc_l, g_l = jnp.split(mod_l[:, None, :], 3, axis=-1)
        sh_c, sc_c, g_c = jnp.split(mod_c, 3)
        pl = (rms_norm(xl, norm_g[l]) * (1.0 + sc_l) + sh_l) @ w_in[l]
        pc = (rms_norm(xc, norm_g[l]) * (1.0 + sc_c) + sh_c) @ w_in[l]
        rnn_p = (rg_conv_w[l], rg_conv_b[l], rg_wa[l], rg_ba[l], rg_wx[l], rg_bx[l], rg_lam[l])
        zero = jnp.zeros((B, D_RNN), jnp.float32)
        hf_c, hb_c = rglru_bidir(pc[..., r0:r1], *rnn_p, zero, zero)
        hf_l, hb_l = rglru_bidir(pl[..., r0:r1], *rnn_p, hf_c[:, -1], hb_c[:, 0])
        filt = (f_w1[l], f_b1[l], f_w2[l], f_b2[l], f_w3[l], f_b3[l], f_w4[l], f_freq[l])
        shared = (hy_conv_w[l], hy_conv_b[l], hy_bias[l], br_norm_h[l], br_norm_r[l], w_out[l])
        y_l = mixer_out(pl, hf_l + hb_l, hyena_filter(N, *filt), *shared)
        if not last:
            y_c = mixer_out(pc, hf_c + hb_c, hyena_filter(L_ctx, *filt), *shared)
            xc = xc + g_c * y_c
        xl = xl + g_l * y_l
    return rms_norm(xl, final_g)
```

```python
import functools
import math

import jax
import jax.numpy as jnp
import numpy as np
from jax import lax
from jax.experimental import pallas as pl
from jax.experimental.pallas import tpu as pltpu

GRID_W = 64
FILTER_TARGET = 1e-2
FAST_DECAY = 0.3
SLOW_DECAY = 1.5
RG_C = 8.0
EPS = 1e-6

F32 = jnp.float32
BF16 = jnp.bfloat16
HIGHEST = lax.Precision.HIGHEST

VMEM_LIMIT_BYTES = 56 * 1024 * 1024


def _params(*sem):
    return pltpu.CompilerParams(dimension_semantics=sem, vmem_limit_bytes=VMEM_LIMIT_BYTES)


def _silu(x):
    return x * jax.nn.sigmoid(x)


def _rms(x, g):
    return x * lax.rsqrt(jnp.mean(x * x, axis=-1, keepdims=True) + EPS) * g


def _mod_kernel(s_ref, w_ref, b_ref, o_ref):
    s = _silu(s_ref[...])
    o_ref[0] = jnp.dot(s, w_ref[0], precision=HIGHEST, preferred_element_type=F32) + b_ref[0]


def _modulation(cond, w_mod, b_mod):
    depth, d, d3 = w_mod.shape
    rows = cond.shape[0]
    tn = d // 2
    return pl.pallas_call(
        _mod_kernel,
        grid=(depth, d3 // tn),
        in_specs=[
            pl.BlockSpec((rows, d), lambda l, j: (0, 0)),
            pl.BlockSpec((1, d, tn), lambda l, j: (l, 0, j)),
            pl.BlockSpec((1, 1, tn), lambda l, j: (l, 0, j)),
        ],
        out_specs=pl.BlockSpec((1, rows, tn), lambda l, j: (l, 0, j)),
        out_shape=jax.ShapeDtypeStruct((depth, rows, d3), F32),
        compiler_params=_params("parallel", "parallel"),
        name="modulation",
    )(cond, w_mod, b_mod.reshape(depth, 1, d3))


def _inproj_kernel(add_pos, *refs):
    if add_pos:
        x_ref, pos_ref, sh_ref, sc_ref, g_ref, w_ref, p_ref, xl_ref = refs
        x = x_ref[0] + pos_ref[...]
        xl_ref[0] = x
    else:
        x_ref, sh_ref, sc_ref, g_ref, w_ref, p_ref = refs
        x = x_ref[0]
    xn = _rms(x, g_ref[...]) * (1.0 + sc_ref[0]) + sh_ref[0]
    p_ref[0] = jnp.dot(xn.astype(BF16), w_ref[...], preferred_element_type=F32)


def _inproj(x, pos, sh, sc, g, w_bf16):
    b, l, d = x.shape
    d_in = w_bf16.shape[1]
    tm = min(l, 512)
    add_pos = pos is not None
    row = pl.BlockSpec((1, tm, d), lambda i, j: (i, j, 0))
    vec = pl.BlockSpec((1, 1, d), lambda i, j: (i, 0, 0))
    in_specs = [row]
    args = [x]
    if add_pos:
        in_specs.append(pl.BlockSpec((tm, d), lambda i, j: (j, 0)))
        args.append(pos)
    in_specs += [vec, vec, pl.BlockSpec((1, d), lambda i, j: (0, 0)),
                 pl.BlockSpec((d, d_in), lambda i, j: (0, 0))]
    args += [sh, sc, g.reshape(1, d), w_bf16]
    p_spec = pl.BlockSpec((1, tm, d_in), lambda i, j: (i, j, 0))
    p_shape = jax.ShapeDtypeStruct((b, l, d_in), F32)
    if add_pos:
        out_specs = [p_spec, row]
        out_shape = [p_shape, jax.ShapeDtypeStruct((b, l, d), F32)]
    else:
        out_specs = p_spec
        out_shape = p_shape
    return pl.pallas_call(
        functools.partial(_inproj_kernel, add_pos),
        grid=(b, l // tm),
        in_specs=in_specs,
        out_specs=out_specs,
        out_shape=out_shape,
        compiler_params=_params("parallel", "parallel"),
        name="inproj_pos" if add_pos else "inproj",
    )(*args)


def _outproj_kernel(final, *refs):
    if final:
        (yh_ref, hf_ref, hb_ref, zh_ref, zr_ref, gnh_ref, gnr_ref, wt_ref, wb_ref,
         x_ref, gate_ref, fg_ref, o_ref) = refs
    else:
        (yh_ref, hf_ref, hb_ref, zh_ref, zr_ref, gnh_ref, gnr_ref, wt_ref, wb_ref,
         x_ref, gate_ref, o_ref) = refs
    yh = _rms(yh_ref[0], gnh_ref[...]) * _silu(zh_ref[0])
    yr = _rms(hf_ref[0] + hb_ref[0], gnr_ref[...]) * _silu(zr_ref[0])
    y = jnp.dot(yh.astype(BF16), wt_ref[...], preferred_element_type=F32)
    y = y + jnp.dot(yr.astype(BF16), wb_ref[...], preferred_element_type=F32)
    out = x_ref[0] + gate_ref[0] * y
    if final:
        out = _rms(out, fg_ref[...])
    o_ref[0] = out


def _outproj(y_h, hf, hb, p, gn_h, gn_r, w_out_bf16, x, gate, final_g):
    b, l, d = x.shape
    dh = y_h.shape[-1]
    dr = hf.shape[-1]
    tm = min(l, 512)
    final = final_g is not None
    assert dh == dr
    zh_blk, zr_blk = 3, 5
    half = lambda w: pl.BlockSpec((1, tm, w), lambda i, j: (i, j, 0))
    in_specs = [
        half(dh), half(dr), half(dr),
        pl.BlockSpec((1, tm, dh), lambda i, j: (i, j, zh_blk)),
        pl.BlockSpec((1, tm, dr), lambda i, j: (i, j, zr_blk)),
        pl.BlockSpec((1, dh), lambda i, j: (0, 0)),
        pl.BlockSpec((1, dr), lambda i, j: (0, 0)),
        pl.BlockSpec((dh, d), lambda i, j: (0, 0)),
        pl.BlockSpec((dr, d), lambda i, j: (0, 0)),
        pl.BlockSpec((1, tm, d), lambda i, j: (i, j, 0)),
        pl.BlockSpec((1, 1, d), lambda i, j: (i, 0, 0)),
    ]
    args = [y_h, hf, hb, p, p, gn_h.reshape(1, dh), gn_r.reshape(1, dr),
            w_out_bf16[:dh], w_out_bf16[dh:], x, gate]
    if final:
        in_specs.append(pl.BlockSpec((1, d), lambda i, j: (0, 0)))
        args.append(final_g.reshape(1, d))
    return pl.pallas_call(
        functools.partial(_outproj_kernel, final),
        grid=(b, l // tm),
        in_specs=in_specs,
        out_specs=pl.BlockSpec((1, tm, d), lambda i, j: (i, j, 0)),
        out_shape=jax.ShapeDtypeStruct((b, l, d), F32),
        compiler_params=_params("parallel", "parallel"),
        name="outproj_final" if final else "outproj",
    )(*args)


def _dwconv(u, w, pad):
    return lax.conv_general_dilated(u, w[:, None, :], window_strides=(1,), padding=[pad],
                                    dimension_numbers=('NWC', 'WIO', 'NWC'),
                                    feature_group_count=u.shape[-1])


def _hyena_filter_jax(L, f_w1, f_b1, f_w2, f_b2, f_w3, f_b3, f_w4, f_freq):
    dh = f_w4.shape[1] // 2
    bands = (f_w1.shape[0] - 1) // 2
    t = jnp.linspace(0.0, 1.0, L, dtype=F32)[:, None]
    w = 2.0 * math.pi * jnp.arange(L, dtype=F32)[:, None] / L
    f = jnp.linspace(1e-4, bands - 1, bands, dtype=F32)[None, :]
    z = jnp.concatenate([t, jnp.cos(f * w), -jnp.sin(f * w)], axis=-1)
    h = jnp.sin(f_freq[0] * (jnp.dot(z, f_w1, precision=HIGHEST) + f_b1))
    h = jnp.sin(f_freq[1] * (jnp.dot(h, f_w2, precision=HIGHEST) + f_b2))
    h = jnp.sin(f_freq[2] * (jnp.dot(h, f_w3, precision=HIGHEST) + f_b3))
    h = jnp.dot(h, f_w4, precision=HIGHEST)
    max_decay = math.log(FILTER_TARGET) / FAST_DECAY
    min_decay = math.log(FILTER_TARGET) / SLOW_DECAY
    deltas = jnp.abs(jnp.linspace(min_decay, max_decay, dh, dtype=F32))
    deltas = jnp.concatenate([deltas, deltas])
    h = h * jnp.exp(-t * deltas[None, :])
    k_f, k_b = h[:, :dh], h[:, dh:]
    l1 = jnp.sum(jnp.abs(k_f), axis=0) + jnp.sum(jnp.abs(k_b[1:]), axis=0)
    k_f = k_f / l1
    k_b = k_b / l1
    return jnp.concatenate([k_f, jnp.zeros((1, dh), F32), k_b[1:][::-1]], axis=0)


def _long_conv_jax(u, kern, bias):
    L = u.shape[1]
    uf = jnp.fft.rfft(u, n=2 * L, axis=1)
    kf = jnp.fft.rfft(kern, n=2 * L, axis=0)
    y = jnp.fft.irfft(uf * kf[None], n=2 * L, axis=1)[:, :L]
    return y + u * bias


def _rglru_coeffs_jax(xc, wa, ba, wx, bx, lam):
    B, L, dr = xc.shape
    heads, blk = wa.shape[0], wa.shape[1]
    xh = xc.reshape(B, L, heads, blk)
    r = jax.nn.sigmoid(jnp.einsum('blhi,hij->blhj', xh, wa).reshape(B, L, dr) + ba)
    i = jax.nn.sigmoid(jnp.einsum('blhi,hij->blhj', xh, wx).reshape(B, L, dr) + bx)
    log_a = -RG_C * r * jax.nn.softplus(-lam)
    a = jnp.exp(log_a)
    b = jnp.sqrt(-jnp.expm1(2.0 * log_a)) * (i * xc)
    return a, b


def _scan_jax(a, b, h0, reverse):
    idx = -1 if reverse else 0
    b = b.at[:, idx].add(a[:, idx] * h0)

    def combine(e1, e2):
        a1, b1 = e1
        a2, b2 = e2
        return a1 * a2, a2 * b1 + b2

    _, h = lax.associative_scan(combine, (a, b), reverse=reverse, axis=1)
    return h


def _rglru_jax(xr, conv_w, conv_b, wa, ba, wx, bx, lam, h0_f, h0_b):
    xc = _dwconv(xr, conv_w, (2, 1)) + conv_b
    a_f, b_f = _rglru_coeffs_jax(xc, wa[0], ba[0], wx[0], bx[0], lam[0])
    a_b, b_b = _rglru_coeffs_jax(xc, wa[1], ba[1], wx[1], bx[1], lam[1])
    return _scan_jax(a_f, b_f, h0_f, False), _scan_jax(a_b, b_b, h0_b, True)


def _hyena_jax(p, kern, hy_conv_w, hy_conv_b, hy_bias):
    dh = hy_bias.shape[0]
    u = _dwconv(p[..., :3 * dh], hy_conv_w, (1, 1)) + hy_conv_b
    x0, x1, v = jnp.split(u, 3, axis=-1)
    return _long_conv_jax(v * x1, kern, hy_bias) * x0


def _grid_sincos(rows, dim):
    r, col = jnp.meshgrid(jnp.arange(rows, dtype=F32), jnp.arange(GRID_W, dtype=F32), indexing='ij')
    quarter = dim // 4
    omega = 1.0 / (10000.0 ** (jnp.arange(quarter, dtype=F32) / quarter))

    def emb(pos):
        ang = pos.reshape(-1, 1) * omega[None, :]
        return jnp.concatenate([jnp.sin(ang), jnp.cos(ang)], axis=-1)

    return jnp.concatenate([emb(r), emb(col)], axis=-1)


def kernel(x, c, ctx, c_ctx, w_mod, b_mod, norm_g, w_in, hy_conv_w, hy_conv_b, f_w1, f_b1, f_w2, f_b2,
           f_w3, f_b3, f_w4, f_freq, hy_bias, rg_conv_w, rg_conv_b, rg_wa, rg_ba, rg_wx, rg_bx, rg_lam,
           br_norm_h, br_norm_r, w_out, final_g):
    B, N, D = x.shape
    L_ctx = ctx.shape[1]
    depth = w_mod.shape[0]
    dh = hy_bias.shape[1]
    dr = rg_conv_b.shape[1]
    r0, r1 = 4 * dh, 4 * dh + dr

    pos = _grid_sincos(N // GRID_W, D)
    cond = jnp.concatenate([c, jnp.broadcast_to(c_ctx[None], (8, D))], axis=0)
    mod = _modulation(cond, w_mod, b_mod)
    w_in_bf = w_in.astype(BF16)
    w_out_bf = w_out.astype(BF16)

    xl, xc = x, ctx
    for l in range(depth):
        last = l == depth - 1
        ml = mod[l, :B].reshape(B, 1, 3 * D)
        mc = jnp.broadcast_to(mod[l, B:B + 1].reshape(1, 1, 3 * D), (B, 1, 3 * D))
        sh_l, sc_l, g_l = ml[..., :D], ml[..., D:2 * D], ml[..., 2 * D:]
        sh_c, sc_c, g_c = mc[..., :D], mc[..., D:2 * D], mc[..., 2 * D:]
        if l == 0:
            p_l, xl = _inproj(xl, pos, sh_l, sc_l, norm_g[l], w_in_bf[l])
        else:
            p_l = _inproj(xl, None, sh_l, sc_l, norm_g[l], w_in_bf[l])
        p_c = _inproj(xc, None, sh_c, sc_c, norm_g[l], w_in_bf[l])
        rnn_p = (rg_conv_w[l], rg_conv_b[l], rg_wa[l], rg_ba[l], rg_wx[l], rg_bx[l], rg_lam[l])
        zero = jnp.zeros((B, dr), F32)
        hf_c, hb_c = _rglru_jax(p_c[..., r0:r1], *rnn_p, zero, zero)
        hf_l, hb_l = _rglru_jax(p_l[..., r0:r1], *rnn_p, hf_c[:, -1], hb_c[:, 0])
        filt = (f_w1[l], f_b1[l], f_w2[l], f_b2[l], f_w3[l], f_b3[l], f_w4[l], f_freq[l])
        yh_l = _hyena_jax(p_l, _hyena_filter_jax(N, *filt), hy_conv_w[l], hy_conv_b[l], hy_bias[l])
        if not last:
            yh_c = _hyena_jax(p_c, _hyena_filter_jax(L_ctx, *filt), hy_conv_w[l], hy_conv_b[l], hy_bias[l])
            xc = _outproj(yh_c, hf_c, hb_c, p_c, br_norm_h[l], br_norm_r[l], w_out_bf[l], xc, g_c, None)
        xl = _outproj(yh_l, hf_l, hb_l, p_l, br_norm_h[l], br_norm_r[l], w_out_bf[l], xl, g_l,
                      final_g if last else None)
    return xl
```

```python
import functools
import math

import jax
import jax.numpy as jnp
import numpy as np
from jax import lax
from jax.experimental import pallas as pl
from jax.experimental.pallas import tpu as pltpu

GRID_W = 64
FILTER_TARGET = 1e-2
FAST_DECAY = 0.3
SLOW_DECAY = 1.5
RG_C = 8.0
EPS = 1e-6

F32 = jnp.float32
BF16 = jnp.bfloat16
HIGHEST = lax.Precision.HIGHEST

VMEM_LIMIT_BYTES = 56 * 1024 * 1024


def _params(*sem):
    return pltpu.CompilerParams(dimension_semantics=sem, vmem_limit_bytes=VMEM_LIMIT_BYTES)


def _silu(x):
    return x * jax.nn.sigmoid(x)


def _rms(x, g):
    return x * lax.rsqrt(jnp.mean(x * x, axis=-1, keepdims=True) + EPS) * g


def _mod_kernel(s_ref, w_ref, b_ref, o_ref):
    s = _silu(s_ref[...])
    o_ref[0] = jnp.dot(s, w_ref[0], precision=HIGHEST, preferred_element_type=F32) + b_ref[0]


def _modulation(cond, w_mod, b_mod):
    depth, d, d3 = w_mod.shape
    rows = cond.shape[0]
    tn = d // 2
    return pl.pallas_call(
        _mod_kernel,
        grid=(depth, d3 // tn),
        in_specs=[
            pl.BlockSpec((rows, d), lambda l, j: (0, 0)),
            pl.BlockSpec((1, d, tn), lambda l, j: (l, 0, j)),
            pl.BlockSpec((1, 1, tn), lambda l, j: (l, 0, j)),
        ],
        out_specs=pl.BlockSpec((1, rows, tn), lambda l, j: (l, 0, j)),
        out_shape=jax.ShapeDtypeStruct((depth, rows, d3), F32),
        compiler_params=_params("parallel", "parallel"),
        name="modulation",
    )(cond, w_mod, b_mod.reshape(depth, 1, d3))


def _inproj_kernel(add_pos, *refs):
    if add_pos:
        x_ref, pos_ref, sh_ref, sc_ref, g_ref, w_ref, p_ref, xl_ref = refs
        x = x_ref[0] + pos_ref[...]
        xl_ref[0] = x
    else:
        x_ref, sh_ref, sc_ref, g_ref, w_ref, p_ref = refs
        x = x_ref[0]
    xn = _rms(x, g_ref[...]) * (1.0 + sc_ref[0]) + sh_ref[0]
    p_ref[0] = jnp.dot(xn.astype(BF16), w_ref[...], preferred_element_type=F32)


def _inproj(x, pos, sh, sc, g, w_bf16):
    b, l, d = x.shape
    d_in = w_bf16.shape[1]
    tm = min(l, 512)
    add_pos = pos is not None
    row = pl.BlockSpec((1, tm, d), lambda i, j: (i, j, 0))
    vec = pl.BlockSpec((1, 1, d), lambda i, j: (i, 0, 0))
    in_specs = [row]
    args = [x]
    if add_pos:
        in_specs.append(pl.BlockSpec((tm, d), lambda i, j: (j, 0)))
        args.append(pos)
    in_specs += [vec, vec, pl.BlockSpec((1, d), lambda i, j: (0, 0)),
                 pl.BlockSpec((d, d_in), lambda i, j: (0, 0))]
    args += [sh, sc, g.reshape(1, d), w_bf16]
    p_spec = pl.BlockSpec((1, tm, d_in), lambda i, j: (i, j, 0))
    p_shape = jax.ShapeDtypeStruct((b, l, d_in), F32)
    if add_pos:
        out_specs = [p_spec, row]
        out_shape = [p_shape, jax.ShapeDtypeStruct((b, l, d), F32)]
    else:
        out_specs = p_spec
        out_shape = p_shape
    return pl.pallas_call(
        functools.partial(_inproj_kernel, add_pos),
        grid=(b, l // tm),
        in_specs=in_specs,
        out_specs=out_specs,
        out_shape=out_shape,
        compiler_params=_params("parallel", "parallel"),
        name="inproj_pos" if add_pos else "inproj",
    )(*args)


def _outproj_kernel(final, *refs):
    if final:
        (cv_ref, u_ref, x0_ref, hyb_ref, hf_ref, hb_ref, zh_ref, zr_ref, gnh_ref, gnr_ref, wt_ref, wb_ref,
         x_ref, gate_ref, fg_ref, o_ref) = refs
    else:
        (cv_ref, u_ref, x0_ref, hyb_ref, hf_ref, hb_ref, zh_ref, zr_ref, gnh_ref, gnr_ref, wt_ref, wb_ref,
         x_ref, gate_ref, o_ref) = refs
    y_h = (cv_ref[0] + u_ref[0] * hyb_ref[...]) * x0_ref[0]
    yh = _rms(y_h, gnh_ref[...]) * _silu(zh_ref[0])
    yr = _rms(hf_ref[0] + hb_ref[0], gnr_ref[...]) * _silu(zr_ref[0])
    y = jnp.dot(yh.astype(BF16), wt_ref[...], preferred_element_type=F32)
    y = y + jnp.dot(yr.astype(BF16), wb_ref[...], preferred_element_type=F32)
    out = x_ref[0] + gate_ref[0] * y
    if final:
        out = _rms(out, fg_ref[...])
    o_ref[0] = out


def _outproj(conv, u, x0, hy_bias, hf, hb, p, gn_h, gn_r, w_out_bf16, x, gate, final_g):
    b, l, d = x.shape
    dh = conv.shape[-1]
    dr = hf.shape[-1]
    tm = min(l, 512)
    final = final_g is not None
    assert dh == dr
    zh_blk, zr_blk = 3, 5
    half = lambda w: pl.BlockSpec((1, tm, w), lambda i, j: (i, j, 0))
    vec = lambda w: pl.BlockSpec((1, w), lambda i, j: (0, 0))
    in_specs = [
        half(dh), half(dh), half(dh), vec(dh), half(dr), half(dr),
        pl.BlockSpec((1, tm, dh), lambda i, j: (i, j, zh_blk)),
        pl.BlockSpec((1, tm, dr), lambda i, j: (i, j, zr_blk)),
        vec(dh), vec(dr),
        pl.BlockSpec((dh, d), lambda i, j: (0, 0)),
        pl.BlockSpec((dr, d), lambda i, j: (0, 0)),
        pl.BlockSpec((1, tm, d), lambda i, j: (i, j, 0)),
        pl.BlockSpec((1, 1, d), lambda i, j: (i, 0, 0)),
    ]
    args = [conv, u, x0, hy_bias.reshape(1, dh), hf, hb, p, p, gn_h.reshape(1, dh), gn_r.reshape(1, dr),
            w_out_bf16[:dh], w_out_bf16[dh:], x, gate]
    if final:
        in_specs.append(vec(d))
        args.append(final_g.reshape(1, d))
    return pl.pallas_call(
        functools.partial(_outproj_kernel, final),
        grid=(b, l // tm),
        in_specs=in_specs,
        out_specs=pl.BlockSpec((1, tm, d), lambda i, j: (i, j, 0)),
        out_shape=jax.ShapeDtypeStruct((b, l, d), F32),
        compiler_params=_params("parallel", "parallel"),
        name="outproj_final" if final else "outproj",
    )(*args)


NB = 8
RG_PREV = 2
RG_NEXT = 1


def _softplus(x):
    return jnp.maximum(x, 0.0) + jnp.log(1.0 + jnp.exp(-jnp.abs(x)))


def _rglru_kernel(tc, xf_ref, xfp_ref, xfn_ref, xb_ref, xbp_ref, xbn_ref, cw_ref, cb_ref,
                  wf_ref, bf_ref, wb_ref, bb_ref, lam_ref, h0f_ref, h0b_ref,
                  hf_ref, hb_ref, af_s, bf_s, ab_s, bb_s, cf_s, cbk_s):
    j = pl.program_id(0)
    nj = pl.num_programs(0)
    rows = tc * NB
    dr = xf_ref.shape[-1]

    @pl.when(j == 0)
    def _():
        cf_s[...] = h0f_ref[...]
        cbk_s[...] = h0b_ref[...]

    def coeffs(x_ref, xp_ref, xn_ref, first, last, w_ref, b_ref, lam, a_s, b_s):
        prev = jnp.where(first, 0.0, xp_ref[...])
        nxt = jnp.where(last, 0.0, xn_ref[...])
        xfull = jnp.concatenate([prev, x_ref[...], nxt], axis=0)
        xc = cb_ref[...] + cw_ref[0:1, :] * xfull[0:rows]
        for k in range(1, RG_PREV + RG_NEXT + 1):
            xc = xc + cw_ref[k:k + 1, :] * xfull[k * NB:k * NB + rows]
        g = jnp.dot(xc.astype(BF16), w_ref[...], preferred_element_type=F32) + b_ref[...]
        r = jax.nn.sigmoid(g[:, :dr])
        i = jax.nn.sigmoid(g[:, dr:])
        a = jnp.exp((-RG_C) * r * _softplus(-lam))
        a_s[...] = a
        b_s[...] = jnp.sqrt(1.0 - a * a) * (i * xc)

    coeffs(xf_ref, xfp_ref, xfn_ref, j == 0, j == nj - 1, wf_ref, bf_ref, lam_ref[0:1, :], af_s, bf_s)
    coeffs(xb_ref, xbp_ref, xbn_ref, j == nj - 1, j == 0, wb_ref, bb_ref, lam_ref[1:2, :], ab_s, bb_s)

    def step(t, carry):
        hf, hb = carry
        rf = pl.multiple_of(t * NB, NB)
        rb = pl.multiple_of((tc - 1 - t) * NB, NB)
        hf = af_s[pl.ds(rf, NB), :] * hf + bf_s[pl.ds(rf, NB), :]
        hb = ab_s[pl.ds(rb, NB), :] * hb + bb_s[pl.ds(rb, NB), :]
        hf_ref[pl.ds(rf, NB), :] = hf
        hb_ref[pl.ds(rb, NB), :] = hb
        return hf, hb

    hf, hb = lax.fori_loop(0, tc, step, (cf_s[...], cbk_s[...]), unroll=8)
    cf_s[...] = hf
    cbk_s[...] = hb


def _rglru(xr2, conv_w, conv_b, w_f, b_f, w_b, b_b, lam, h0_f, h0_b):
    rows_total, dr = xr2.shape
    l = rows_total // NB
    tc = min(l, 128)
    nj = l // tc
    rows = tc * NB
    prev_rows = RG_PREV * NB
    last_prev = rows // prev_rows
    n_next = rows_total // NB

    def blk(shape, fn):
        return pl.BlockSpec(shape, fn)

    full = lambda shape: pl.BlockSpec(shape, lambda j: tuple(0 for _ in shape))
    in_specs = [
        blk((rows, dr), lambda j: (j, 0)),
        blk((prev_rows, dr), lambda j: (jnp.maximum(j * last_prev - 1, 0), 0)),
        blk((NB, dr), lambda j: (jnp.minimum((j + 1) * tc, n_next - 1), 0)),
        blk((rows, dr), lambda j: (nj - 1 - j, 0)),
        blk((prev_rows, dr), lambda j: (jnp.maximum((nj - 1 - j) * last_prev - 1, 0), 0)),
        blk((NB, dr), lambda j: (jnp.minimum((nj - j) * tc, n_next - 1), 0)),
        full(conv_w.shape), full((1, dr)),
        full(w_f.shape), full((1, 2 * dr)), full(w_b.shape), full((1, 2 * dr)),
        full(lam.shape), full((NB, dr)), full((NB, dr)),
    ]
    out_specs = [blk((rows, dr), lambda j: (j, 0)), blk((rows, dr), lambda j: (nj - 1 - j, 0))]
    out_shape = [jax.ShapeDtypeStruct((rows_total, dr), F32)] * 2
    scratch = [pltpu.VMEM((rows, dr), F32)] * 4 + [pltpu.VMEM((NB, dr), F32)] * 2
    return pl.pallas_call(
        functools.partial(_rglru_kernel, tc),
        grid=(nj,),
        in_specs=in_specs,
        out_specs=out_specs,
        out_shape=out_shape,
        scratch_shapes=scratch,
        compiler_params=_params("arbitrary"),
        name="rglru",
    )(xr2, xr2, xr2, xr2, xr2, xr2, conv_w, conv_b.reshape(1, dr), w_f, b_f, w_b, b_b, lam, h0_f, h0_b)


def _block_diag(w):
    heads, blk, _ = w.shape
    eye = jnp.eye(heads, dtype=w.dtype)
    return jnp.einsum('hij,hg->higj', w, eye).reshape(heads * blk, heads * blk)


def _rglru_weights(wa, ba, wx, bx):
    out = []
    for d in range(2):
        w = jnp.concatenate([_block_diag(wa[d]), _block_diag(wx[d])], axis=1).astype(BF16)
        b = jnp.concatenate([ba[d], bx[d]])[None, :]
        out += [w, b]
    return out


LANES = 128
TW_B = 8


def _fft_dims(l):
    n = 2 * l
    n1 = 16
    while (2 * n1) * (2 * n1) <= n:
        n1 *= 2
    return n1, n // n1


def _hilo(m):
    m32 = jnp.asarray(m, F32)
    hi = m32.astype(BF16)
    lo = (m32 - hi.astype(F32)).astype(BF16)
    return jnp.stack([hi, lo])


def _fft_consts(l):
    n1, n2 = _fft_dims(l)
    n = n1 * n2
    h = n1 // 2
    f1 = np.exp(-2j * np.pi * np.outer(np.arange(n1), np.arange(n1)) / n1)
    f2 = np.exp(-2j * np.pi * np.outer(np.arange(n2), np.arange(n2)) / n2)
    stack = lambda a: np.block([[a.real, -a.imag], [a.imag, a.real]])
    m1_full = stack(f1)
    m1_half = np.concatenate([m1_full[:, :h], m1_full[:, n1:n1 + h]], axis=1)
    m2 = stack(f2)
    m2i = stack(np.conj(f2))
    m3_full = stack(np.conj(f1))
    m3 = np.concatenate([m3_full[:h], m3_full[n1:n1 + h]], axis=0)
    na = n1 // TW_B
    ang_a = -2.0 * np.pi * np.outer(np.arange(na) * TW_B, np.arange(n2)) / n
    ang_b = -2.0 * np.pi * np.outer(np.arange(TW_B), np.arange(n2)) / n
    lanes = lambda a: jnp.asarray(np.broadcast_to(a[..., None], a.shape + (LANES,)), F32)
    ta = jnp.stack([lanes(np.cos(ang_a)), lanes(np.sin(ang_a))])
    tb = jnp.stack([lanes(np.cos(ang_b)), lanes(np.sin(ang_b))])
    return dict(n1=n1, n2=n2, m1_half=_hilo(m1_half), m1_real=_hilo(m1_full[:, :n1]),
                m2=_hilo(m2), m2i=_hilo(m2i), m3=_hilo(m3), ta=ta, tb=tb)


def _mm3(m_ref, x):
    xh = x.astype(BF16)
    xl = (x - xh.astype(F32)).astype(BF16)
    mh = m_ref[0]
    out = jnp.dot(mh, xh, preferred_element_type=F32)
    out = out + jnp.dot(mh, xl, preferred_element_type=F32)
    return out + jnp.dot(m_ref[1], xh, preferred_element_type=F32)


def _twiddle(ta_ref, tb_ref, k1):
    a = k1 // TW_B
    b = k1 % TW_B
    ar, ai, br, bi = ta_ref[0, a], ta_ref[1, a], tb_ref[0, b], tb_ref[1, b]
    return ar * br - ai * bi, ar * bi + ai * br


def _fwd_stage2(n2, k1, a_ref, m2_ref, ta_ref, tb_ref):
    xs, tws = [], []
    for g in range(2):
        r0 = pl.multiple_of((k1 + g) * n2, n2)
        ar, ai = a_ref[0, pl.ds(r0, n2), :], a_ref[1, pl.ds(r0, n2), :]
        tr, ti = _twiddle(ta_ref, tb_ref, k1 + g)
        xs.append(jnp.concatenate([ar * tr - ai * ti, ar * ti + ai * tr], axis=0))
        tws.append((tr, ti))
    return _mm3(m2_ref, jnp.concatenate(xs, axis=1)), tws


def _fftconv_kernel(n1, n2, u_ref, ks_ref, m1_ref, m2_ref, m2i_ref, m3_ref, ta_ref, tb_ref, y_ref, a_ref):
    h = n1 // 2

    def phase1(i, carry):
        cols = []
        for g in range(2):
            zr = u_ref[0, pl.ds(2 * i + g, h, stride=n2), :]
            zi = u_ref[1, pl.ds(2 * i + g, h, stride=n2), :]
            cols.append(jnp.concatenate([zr, zi], axis=0))
        out = _mm3(m1_ref, jnp.concatenate(cols, axis=1))
        for g in range(2):
            a_ref[0, pl.ds(2 * i + g, n1, stride=n2), :] = out[:n1, g * LANES:(g + 1) * LANES]
            a_ref[1, pl.ds(2 * i + g, n1, stride=n2), :] = out[n1:, g * LANES:(g + 1) * LANES]
        return carry

    lax.fori_loop(0, n2 // 2, phase1, 0)

    def phase2(i, carry):
        k1 = 2 * i
        b, tws = _fwd_stage2(n2, k1, a_ref, m2_ref, ta_ref, tb_ref)
        cs = []
        for g in range(2):
            r0 = pl.multiple_of((k1 + g) * n2, n2)
            br, bi = b[:n2, g * LANES:(g + 1) * LANES], b[n2:, g * LANES:(g + 1) * LANES]
            kr, ki = ks_ref[0, pl.ds(r0, n2), :], ks_ref[1, pl.ds(r0, n2), :]
            cs.append(jnp.concatenate([br * kr - bi * ki, br * ki + bi * kr], axis=0))
        d = _mm3(m2i_ref, jnp.concatenate(cs, axis=1))
        for g in range(2):
            r0 = pl.multiple_of((k1 + g) * n2, n2)
            dr, di = d[:n2, g * LANES:(g + 1) * LANES], d[n2:, g * LANES:(g + 1) * LANES]
            tr, ti = tws[g]
            a_ref[0, pl.ds(r0, n2), :] = dr * tr + di * ti
            a_ref[1, pl.ds(r0, n2), :] = di * tr - dr * ti
        return carry

    lax.fori_loop(0, n1 // 2, phase2, 0)

    def phase3(i, carry):
        cols = []
        for g in range(2):
            er = a_ref[0, pl.ds(2 * i + g, n1, stride=n2), :]
            ei = a_ref[1, pl.ds(2 * i + g, n1, stride=n2), :]
            cols.append(jnp.concatenate([er, ei], axis=0))
        out = _mm3(m3_ref, jnp.concatenate(cols, axis=1))
        for g in range(2):
            y_ref[0, pl.ds(2 * i + g, h, stride=n2), :] = out[:h, g * LANES:(g + 1) * LANES]
            y_ref[1, pl.ds(2 * i + g, h, stride=n2), :] = out[h:, g * LANES:(g + 1) * LANES]
        return carry

    lax.fori_loop(0, n2 // 2, phase3, 0)


def _const_spec(a):
    return pl.BlockSpec(a.shape, lambda *_: tuple(0 for _ in a.shape), pipeline_mode=pl.Buffered(1))


def _fftconv(u, kspec, fc):
    b, l, dh = u.shape
    n1, n2 = fc["n1"], fc["n2"]
    n = n1 * n2
    consts = [fc["m1_half"], fc["m2"], fc["m2i"], fc["m3"], fc["ta"], fc["tb"]]
    return pl.pallas_call(
        functools.partial(_fftconv_kernel, n1, n2),
        grid=(dh // LANES, b // 2),
        in_specs=[
            pl.BlockSpec((2, l, LANES), lambda c, i: (i, 0, c), pipeline_mode=pl.Buffered(1)),
            pl.BlockSpec((2, n, LANES), lambda c, i: (0, 0, c), pipeline_mode=pl.Buffered(1)),
        ] + [_const_spec(a) for a in consts],
        out_specs=pl.BlockSpec((2, l, LANES), lambda c, i: (i, 0, c), pipeline_mode=pl.Buffered(1)),
        out_shape=jax.ShapeDtypeStruct((b, l, dh), F32),
        scratch_shapes=[pltpu.VMEM((2, n, LANES), F32)],
        compiler_params=_params("parallel", "arbitrary"),
        name="fftconv",
    )(u, kspec, *consts)


def _filter_fft_kernel(n1, n2, k_ref, l1_ref, m1_ref, m2_ref, ta_ref, tb_ref, o_ref):
    scale = (1.0 / (n1 * n2)) / l1_ref[...]

    def phase1(i, carry):
        cols = [k_ref[pl.ds(2 * i + g, n1, stride=n2), :] for g in range(2)]
        out = _mm3(m1_ref, jnp.concatenate(cols, axis=1))
        for g in range(2):
            o_ref[0, pl.ds(2 * i + g, n1, stride=n2), :] = out[:n1, g * LANES:(g + 1) * LANES]
            o_ref[1, pl.ds(2 * i + g, n1, stride=n2), :] = out[n1:, g * LANES:(g + 1) * LANES]
        return carry

    lax.fori_loop(0, n2 // 2, phase1, 0)

    def phase2(i, carry):
        k1 = 2 * i
        b, _ = _fwd_stage2(n2, k1, o_ref, m2_ref, ta_ref, tb_ref)
        for g in range(2):
            r0 = pl.multiple_of((k1 + g) * n2, n2)
            o_ref[0, pl.ds(r0, n2), :] = b[:n2, g * LANES:(g + 1) * LANES] * scale
            o_ref[1, pl.ds(r0, n2), :] = b[n2:, g * LANES:(g + 1) * LANES] * scale
        return carry

    lax.fori_loop(0, n1 // 2, phase2, 0)


def _filter_fft(kern, l1, fc):
    n, dh = kern.shape
    n1, n2 = fc["n1"], fc["n2"]
    consts = [fc["m1_real"], fc["m2"], fc["ta"], fc["tb"]]
    return pl.pallas_call(
        functools.partial(_filter_fft_kernel, n1, n2),
        grid=(dh // LANES,),
        in_specs=[pl.BlockSpec((n, LANES), lambda c: (0, c), pipeline_mode=pl.Buffered(1)),
                  pl.BlockSpec((1, LANES), lambda c: (0, c))]
        + [_const_spec(a) for a in consts],
        out_specs=pl.BlockSpec((2, n, LANES), lambda c: (0, 0, c)),
        out_shape=jax.ShapeDtypeStruct((2, n, dh), F32),
        compiler_params=_params("parallel"),
        name="filter_fft",
    )(kern, l1, *consts)


def _filter_mlp_kernel(z_ref, w1_ref, b1_ref, w2_ref, b2_ref, w3_ref, b3_ref, fr_ref, h_ref):
    dot = lambda a, b: jnp.dot(a, b, precision=HIGHEST, preferred_element_type=F32)
    h = jnp.sin(fr_ref[0:1, :] * (dot(z_ref[...], w1_ref[...]) + b1_ref[...]))
    h = jnp.sin(fr_ref[1:2, :] * (dot(h, w2_ref[...]) + b2_ref[...]))
    h_ref[...] = jnp.sin(fr_ref[2:3, :] * (dot(h, w3_ref[...]) + b3_ref[...]))


def _filter_out_kernel(l, tr, h_ref, w4f_ref, w4b_ref, dl_ref, o_ref, l1_ref):
    rc = pl.program_id(1)
    hcat = h_ref[...]
    hf = jnp.dot(hcat, w4f_ref[...], precision=HIGHEST, preferred_element_type=F32)
    hb = jnp.dot(hcat, w4b_ref[...], precision=HIGHEST, preferred_element_type=F32)
    row = lax.broadcasted_iota(jnp.int32, (tr, LANES), 0) + rc * tr
    step = 1.0 / (l - 1)
    t_f = row.astype(F32) * step
    t_b = (l - row).astype(F32) * step
    kf = hf * jnp.exp(-t_f * dl_ref[...])
    kb = jnp.where(row == 0, 0.0, hb * jnp.exp(-t_b * dl_ref[...]))
    o_ref[0] = kf
    o_ref[1] = kb
    part = jnp.sum(jnp.abs(kf), axis=0, keepdims=True) + jnp.sum(jnp.abs(kb), axis=0, keepdims=True)

    @pl.when(rc == 0)
    def _():
        l1_ref[...] = part

    @pl.when(rc > 0)
    def _():
        l1_ref[...] += part


def _filter_feats(l, bands):
    t = jnp.linspace(0.0, 1.0, l, dtype=F32)[:, None]
    w = 2.0 * math.pi * jnp.arange(l, dtype=F32)[:, None] / l
    f = jnp.linspace(1e-4, bands - 1, bands, dtype=F32)[None, :]
    z = jnp.concatenate([t, jnp.cos(f * w), -jnp.sin(f * w)], axis=-1)
    z_rev = jnp.concatenate([z[:1], z[1:][::-1]], axis=0)
    return jnp.concatenate([z, z_rev], axis=-1)


def _hyena_filter(l, f_w1, f_b1, f_w2, f_b2, f_w3, f_b3, f_w4, f_freq):
    emb, hid = f_w1.shape
    dh = f_w4.shape[1] // 2
    tr = min(l, 1024)
    z2 = _filter_feats(l, (emb - 1) // 2)
    bd = lambda w: jnp.kron(jnp.eye(2, dtype=F32), w)
    two = lambda v: jnp.concatenate([v, v], axis=-1)
    margs = [z2, bd(f_w1), two(f_b1)[None], bd(f_w2), two(f_b2)[None], bd(f_w3), two(f_b3)[None], two(f_freq)]
    full1 = lambda a: pl.BlockSpec(a.shape, lambda r: tuple(0 for _ in a.shape))
    hcat = pl.pallas_call(
        _filter_mlp_kernel,
        grid=(l // tr,),
        in_specs=[pl.BlockSpec((tr, 2 * emb), lambda r: (r, 0))] + [full1(a) for a in margs[1:]],
        out_specs=pl.BlockSpec((tr, 2 * hid), lambda r: (r, 0)),
        out_shape=jax.ShapeDtypeStruct((l, 2 * hid), F32),
        compiler_params=_params("parallel"),
        name="filter_mlp",
    )(*margs)
    zeros = jnp.zeros((hid, dh), F32)
    w4f = jnp.concatenate([f_w4[:, :dh], zeros], axis=0)
    w4b = jnp.concatenate([zeros, f_w4[:, dh:]], axis=0)
    max_decay = math.log(FILTER_TARGET) / FAST_DECAY
    min_decay = math.log(FILTER_TARGET) / SLOW_DECAY
    deltas = jnp.abs(jnp.linspace(min_decay, max_decay, dh, dtype=F32))[None, :]
    kern, l1 = pl.pallas_call(
        functools.partial(_filter_out_kernel, l, tr),
        grid=(dh // LANES, l // tr),
        in_specs=[
            pl.BlockSpec((tr, 2 * hid), lambda c, r: (r, 0)),
            pl.BlockSpec((2 * hid, LANES), lambda c, r: (0, c)),
            pl.BlockSpec((2 * hid, LANES), lambda c, r: (0, c)),
            pl.BlockSpec((1, LANES), lambda c, r: (0, c)),
        ],
        out_specs=[pl.BlockSpec((2, tr, LANES), lambda c, r: (0, r, c)),
                   pl.BlockSpec((1, LANES), lambda c, r: (0, c))],
        out_shape=[jax.ShapeDtypeStruct((2, l, dh), F32), jax.ShapeDtypeStruct((1, dh), F32)],
        compiler_params=_params("parallel", "arbitrary"),
        name="filter_out",
    )(hcat, w4f, w4b, deltas)
    return kern.reshape(2 * l, dh), l1


def _hyena_pre_kernel(tm, dh, x_ref, xp_ref, xn_ref, w_ref, b_ref, u_ref, x0_ref):
    j = pl.program_id(1)
    prev = jnp.where(j == 0, 0.0, xp_ref[0])
    nxt = jnp.where(j == pl.num_programs(1) - 1, 0.0, xn_ref[0])
    xfull = jnp.concatenate([prev, x_ref[0], nxt], axis=0)
    c = b_ref[...] + w_ref[0:1, :] * xfull[7:7 + tm]
    c = c + w_ref[1:2, :] * xfull[8:8 + tm]
    c = c + w_ref[2:3, :] * xfull[9:9 + tm]
    x0_ref[0] = c[:, :dh]
    u_ref[0] = c[:, 2 * dh:] * c[:, dh:2 * dh]


def _hyena_pre(p, conv_w, conv_b):
    b, l, _ = p.shape
    dh3 = conv_w.shape[1]
    dh = dh3 // 3
    tm = min(l, 256)
    nt = l // 8
    out = jax.ShapeDtypeStruct((b, l, dh), F32)
    return pl.pallas_call(
        functools.partial(_hyena_pre_kernel, tm, dh),
        grid=(b, l // tm),
        in_specs=[
            pl.BlockSpec((1, tm, dh3), lambda i, j: (i, j, 0)),
            pl.BlockSpec((1, 8, dh3), lambda i, j: (i, jnp.maximum(j * (tm // 8) - 1, 0), 0)),
            pl.BlockSpec((1, 8, dh3), lambda i, j: (i, jnp.minimum((j + 1) * (tm // 8), nt - 1), 0)),
            pl.BlockSpec(conv_w.shape, lambda i, j: (0, 0)),
            pl.BlockSpec((1, dh3), lambda i, j: (0, 0)),
        ],
        out_specs=[pl.BlockSpec((1, tm, dh), lambda i, j: (i, j, 0))] * 2,
        out_shape=[out, out],
        compiler_params=_params("parallel", "parallel"),
        name="hyena_pre",
    )(p, p, p, conv_w, conv_b.reshape(1, dh3))


def _grid_sincos(rows, dim):
    r, col = jnp.meshgrid(jnp.arange(rows, dtype=F32), jnp.arange(GRID_W, dtype=F32), indexing='ij')
    quarter = dim // 4
    omega = 1.0 / (10000.0 ** (jnp.arange(quarter, dtype=F32) / quarter))

    def emb(pos):
        ang = pos.reshape(-1, 1) * omega[None, :]
        return jnp.concatenate([jnp.sin(ang), jnp.cos(ang)], axis=-1)

    return jnp.concatenate([emb(r), emb(col)], axis=-1)


def kernel(x, c, ctx, c_ctx, w_mod, b_mod, norm_g, w_in, hy_conv_w, hy_conv_b, f_w1, f_b1, f_w2, f_b2,
           f_w3, f_b3, f_w4, f_freq, hy_bias, rg_conv_w, rg_conv_b, rg_wa, rg_ba, rg_wx, rg_bx, rg_lam,
           br_norm_h, br_norm_r, w_out, final_g):
    B, N, D = x.shape
    L_ctx = ctx.shape[1]
    depth = w_mod.shape[0]
    dh = hy_bias.shape[1]
    dr = rg_conv_b.shape[1]
    r0, r1 = 4 * dh, 4 * dh + dr
    assert B == NB

    pos = _grid_sincos(N // GRID_W, D)
    cond = jnp.concatenate([c, jnp.broadcast_to(c_ctx[None], (8, D))], axis=0)
    mod = _modulation(cond, w_mod, b_mod)
    w_in_bf = w_in.astype(BF16)
    w_out_bf = w_out.astype(BF16)
    fc_lat = _fft_consts(N)
    fc_ctx = _fft_consts(L_ctx)

    xl, xc = x, ctx
    for l in range(depth):
        last = l == depth - 1
        ml = mod[l, :B].reshape(B, 1, 3 * D)
        mc = jnp.broadcast_to(mod[l, B:B + 1].reshape(1, 1, 3 * D), (B, 1, 3 * D))
        sh_l, sc_l, g_l = ml[..., :D], ml[..., D:2 * D], ml[..., 2 * D:]
        sh_c, sc_c, g_c = mc[..., :D], mc[..., D:2 * D], mc[..., 2 * D:]
        if l == 0:
            p_l, xl = _inproj(xl, pos, sh_l, sc_l, norm_g[l], w_in_bf[l])
        else:
            p_l = _inproj(xl, None, sh_l, sc_l, norm_g[l], w_in_bf[l])
        p_c = _inproj(xc, None, sh_c, sc_c, norm_g[l], w_in_bf[l])
        rnn_w = _rglru_weights(rg_wa[l], rg_ba[l], rg_wx[l], rg_bx[l])
        zero = jnp.zeros((B, dr), F32)
        to_tm = lambda a: a.transpose(1, 0, 2).reshape(-1, a.shape[-1])
        from_tm = lambda a: a.reshape(-1, B, a.shape[-1]).transpose(1, 0, 2)
        hf_c2, hb_c2 = _rglru(to_tm(p_c[..., r0:r1]), rg_conv_w[l], rg_conv_b[l], *rnn_w, rg_lam[l], zero, zero)
        hf_l2, hb_l2 = _rglru(to_tm(p_l[..., r0:r1]), rg_conv_w[l], rg_conv_b[l], *rnn_w, rg_lam[l],
                              hf_c2[-NB:], hb_c2[:NB])
        hf_c, hb_c, hf_l, hb_l = from_tm(hf_c2), from_tm(hb_c2), from_tm(hf_l2), from_tm(hb_l2)
        filt = (f_w1[l], f_b1[l], f_w2[l], f_b2[l], f_w3[l], f_b3[l], f_w4[l], f_freq[l])
        shared = (br_norm_h[l], br_norm_r[l], w_out_bf[l])
        u_l, x0_l = _hyena_pre(p_l, hy_conv_w[l], hy_conv_b[l])
        cv_l = _fftconv(u_l, _filter_fft(*_hyena_filter(N, *filt), fc_lat), fc_lat)
        if not last:
            u_c, x0_c = _hyena_pre(p_c, hy_conv_w[l], hy_conv_b[l])
            cv_c = _fftconv(u_c, _filter_fft(*_hyena_filter(L_ctx, *filt), fc_ctx), fc_ctx)
            xc = _outproj(cv_c, u_c, x0_c, hy_bias[l], hf_c, hb_c, p_c, *shared, xc, g_c, None)
        xl = _outproj(cv_l, u_l, x0_l, hy_bias[l], hf_l, hb_l, p_l, *shared, xl, g_l,
                      final_g if last else None)
    return xl
```

```python
import functools
import math

import jax
import jax.numpy as jnp
import numpy as np
from jax import lax
from jax.experimental import pallas as pl
from jax.experimental.pallas import tpu as pltpu

GRID_W = 64
FILTER_TARGET = 1e-2
FAST_DECAY = 0.3
SLOW_DECAY = 1.5
RG_C = 8.0
EPS = 1e-6

LANES = 128
F32 = jnp.float32
BF16 = jnp.bfloat16
HIGHEST = lax.Precision.HIGHEST

VMEM_LIMIT_BYTES = 56 * 1024 * 1024


def _params(*sem):
    return pltpu.CompilerParams(dimension_semantics=sem, vmem_limit_bytes=VMEM_LIMIT_BYTES)


def _silu(x):
    return x * jax.nn.sigmoid(x)


def _rms(x, g):
    return x * lax.rsqrt(jnp.mean(x * x, axis=-1, keepdims=True) + EPS) * g


def _mod_kernel(s_ref, w_ref, b_ref, o_ref):
    s = _silu(s_ref[...])
    o_ref[0] = jnp.dot(s, w_ref[0], precision=HIGHEST, preferred_element_type=F32) + b_ref[0]


def _modulation(cond, w_mod, b_mod):
    depth, d, d3 = w_mod.shape
    rows = cond.shape[0]
    tn = d // 2
    return pl.pallas_call(
        _mod_kernel,
        grid=(depth, d3 // tn),
        in_specs=[
            pl.BlockSpec((rows, d), lambda l, j: (0, 0)),
            pl.BlockSpec((1, d, tn), lambda l, j: (l, 0, j)),
            pl.BlockSpec((1, 1, tn), lambda l, j: (l, 0, j)),
        ],
        out_specs=pl.BlockSpec((1, rows, tn), lambda l, j: (l, 0, j)),
        out_shape=jax.ShapeDtypeStruct((depth, rows, d3), F32),
        compiler_params=_params("parallel", "parallel"),
        name="modulation",
    )(cond, w_mod, b_mod.reshape(depth, 1, d3))


NB = 8
HALO = 8
BB = 4


def _inproj_kernel(add_pos, tm, dh, dr, n2, *refs):
    refs = list(refs)
    x_ref, xp_ref, xn_ref = refs[:3]
    refs = refs[3:]
    if add_pos:
        pos_ref, posp_ref, posn_ref = refs[:3]
        refs = refs[3:]
    (sh_ref, sc_ref, g_ref, wh_ref, wr_ref, cw_ref, cb_ref, u_ref, x0_ref, zh_ref, zr_ref, xr_ref) = refs[:12]
    j = pl.program_id(0)
    i = pl.program_id(1)
    keep_prev = (j > 0).astype(F32)
    keep_next = (j < pl.num_programs(0) - 1).astype(F32)
    seg = tm + 2 * HALO

    xs = []
    for b in range(BB):
        xb = jnp.concatenate([xp_ref[b], x_ref[b], xn_ref[b]], axis=0)
        if add_pos:
            xb = xb + jnp.concatenate([posp_ref[...], pos_ref[...], posn_ref[...]], axis=0)
            refs[12][b] = xb[HALO:HALO + tm]
        xs.append((_rms(xb, g_ref[...]) * (1.0 + sc_ref[b]) + sh_ref[b]).astype(BF16))
    xn_all = jnp.concatenate(xs, axis=0)

    ph = jnp.dot(xn_all, wh_ref[...], preferred_element_type=F32)
    for b in range(BB):
        pb = ph[b * seg:(b + 1) * seg]
        full = jnp.concatenate([pb[:HALO] * keep_prev, pb[HALO:HALO + tm], pb[HALO + tm:] * keep_next], axis=0)
        c = cb_ref[...] + cw_ref[0:1, :] * full[HALO - 1:HALO - 1 + tm]
        c = c + cw_ref[1:2, :] * full[HALO:HALO + tm]
        c = c + cw_ref[2:3, :] * full[HALO + 1:HALO + 1 + tm]
        x0_ref[b] = c[:, :dh]

        def store(r, n, v, b=b):
            u_ref[b, r:r + n, :] = v
        _store_pitched(store, c[:, 2 * dh:] * c[:, dh:2 * dh], n2)

    xn_mid = jnp.concatenate([x[HALO:HALO + tm] for x in xs], axis=0)
    pr = jnp.dot(xn_mid, wr_ref[...], preferred_element_type=F32)
    for b in range(BB):
        pb = pr[b * tm:(b + 1) * tm]
        zh_ref[b] = pb[:, :dh]
        zr_ref[b] = pb[:, dh + dr:]
        for s_ in range(dr // LANES):
            xr_ref[s_, pl.ds(i * BB + b, tm, stride=NB), :] = pb[:, dh + s_ * LANES:dh + (s_ + 1) * LANES]


def _inproj(x, pos, sh, sc, g, w_bf16, conv_w, conv_b, dh, dr, n2):
    b, l, d = x.shape
    tm = min(l, 128)
    add_pos = pos is not None
    nt = l // HALO
    per = tm // HALO
    row = lambda w: pl.BlockSpec((BB, tm, w), lambda j, i: (i, j, 0))
    prev = lambda j: jnp.maximum(j * per - 1, 0)
    nxt = lambda j: jnp.minimum((j + 1) * per, nt - 1)
    vec = pl.BlockSpec((BB, 1, d), lambda j, i: (i, 0, 0))
    full = lambda a: pl.BlockSpec(a.shape, lambda j, i: tuple(0 for _ in a.shape), pipeline_mode=pl.Buffered(1))
    in_specs = [row(d),
                pl.BlockSpec((BB, HALO, d), lambda j, i: (i, prev(j), 0)),
                pl.BlockSpec((BB, HALO, d), lambda j, i: (i, nxt(j), 0))]
    args = [x, x, x]
    if add_pos:
        in_specs += [pl.BlockSpec((tm, d), lambda j, i: (j, 0)),
                     pl.BlockSpec((HALO, d), lambda j, i: (prev(j), 0)),
                     pl.BlockSpec((HALO, d), lambda j, i: (nxt(j), 0))]
        args += [pos, pos, pos]
    w_h = w_bf16[:, :3 * dh]
    w_r = w_bf16[:, 3 * dh:]
    consts = [g.reshape(1, d), w_h, w_r, conv_w, conv_b.reshape(1, 3 * dh)]
    in_specs += [vec, vec] + [full(a) for a in consts]
    args += [sh, sc] + consts
    out_specs = [pl.BlockSpec((BB, _pitched_rows(tm, n2), dh), lambda j, i: (i, j, 0)),
                 row(dh), row(dh), row(dr),
                 pl.BlockSpec((dr // LANES, tm * NB, LANES), lambda j, i: (0, j, 0))]
    out_shape = [jax.ShapeDtypeStruct((b, _pitched_rows(l, n2), dh), F32),
                 jax.ShapeDtypeStruct((b, l, dh), F32), jax.ShapeDtypeStruct((b, l, dh), F32),
                 jax.ShapeDtypeStruct((b, l, dr), F32),
                 jax.ShapeDtypeStruct((dr // LANES, l * NB, LANES), F32)]
    if add_pos:
        out_specs.append(row(d))
        out_shape.append(jax.ShapeDtypeStruct((b, l, d), F32))
    return pl.pallas_call(
        functools.partial(_inproj_kernel, add_pos, tm, dh, dr, n2),
        grid=(l // tm, b // BB),
        in_specs=in_specs,
        out_specs=out_specs,
        out_shape=out_shape,
        compiler_params=_params("arbitrary", "arbitrary"),
        name="inproj_pos" if add_pos else "inproj",
    )(*args)


def _outproj_kernel(final, tm, n2, *refs):
    if final:
        (cv_ref, u_ref, x0_ref, hyb_ref, hf_ref, hb_ref, zh_ref, zr_ref, gnh_ref, gnr_ref, wt_ref, wb_ref,
         x_ref, gate_ref, fg_ref, o_ref) = refs
    else:
        (cv_ref, u_ref, x0_ref, hyb_ref, hf_ref, hb_ref, zh_ref, zr_ref, gnh_ref, gnr_ref, wt_ref, wb_ref,
         x_ref, gate_ref, o_ref) = refs
    i = pl.program_id(1)
    yhs, yrs = [], []
    for b in range(BB):
        cv = _load_pitched(lambda r, n, b=b: cv_ref[b, r:r + n, :], tm, n2)
        u = _load_pitched(lambda r, n, b=b: u_ref[b, r:r + n, :], tm, n2)
        y_h = (cv + u * hyb_ref[...]) * x0_ref[b]
        yhs.append((_rms(y_h, gnh_ref[...]) * _silu(zh_ref[b])).astype(BF16))
        hs = _lane_cat([hf_ref[s_, pl.ds(i * BB + b, tm, stride=NB), :] + hb_ref[s_, pl.ds(i * BB + b, tm, stride=NB), :]
                        for s_ in range(hf_ref.shape[0])])
        yrs.append((_rms(hs, gnr_ref[...]) * _silu(zr_ref[b])).astype(BF16))
    y = jnp.dot(jnp.concatenate(yhs, axis=0), wt_ref[...], preferred_element_type=F32)
    y = y + jnp.dot(jnp.concatenate(yrs, axis=0), wb_ref[...], preferred_element_type=F32)
    for b in range(BB):
        out = x_ref[b] + gate_ref[b] * y[b * tm:(b + 1) * tm]
        if final:
            out = _rms(out, fg_ref[...])
        o_ref[b] = out


def _outproj(conv, u, x0, hy_bias, n2, hf, hb, zh, zr, gn_h, gn_r, w_out_bf16, x, gate, final_g):
    b, l, d = x.shape
    dh = conv.shape[-1]
    dr = zr.shape[-1]
    tm = min(l, 128)
    final = final_g is not None
    row = lambda w: pl.BlockSpec((BB, tm, w), lambda j, i: (i, j, 0))
    pitched = pl.BlockSpec((BB, _pitched_rows(tm, n2), dh), lambda j, i: (i, j, 0))
    slabs = pl.BlockSpec((dr // LANES, tm * NB, LANES), lambda j, i: (0, j, 0))
    full = lambda a: pl.BlockSpec(a.shape, lambda j, i: tuple(0 for _ in a.shape), pipeline_mode=pl.Buffered(1))
    consts = [gn_h.reshape(1, dh), gn_r.reshape(1, dr), w_out_bf16[:dh], w_out_bf16[dh:]]
    hyb = hy_bias.reshape(1, dh)
    in_specs = [pitched, pitched, row(dh), full(hyb), slabs, slabs, row(dh), row(dr)] + [full(a) for a in consts] + [
        row(d), pl.BlockSpec((BB, 1, d), lambda j, i: (i, 0, 0))]
    args = [conv, u, x0, hyb, hf, hb, zh, zr] + consts + [x, gate]
    if final:
        fg = final_g.reshape(1, d)
        in_specs.append(full(fg))
        args.append(fg)
    return pl.pallas_call(
        functools.partial(_outproj_kernel, final, tm, n2),
        grid=(l // tm, b // BB),
        in_specs=in_specs,
        out_specs=row(d),
        out_shape=jax.ShapeDtypeStruct((b, l, d), F32),
        compiler_params=_params("parallel", "parallel"),
        name="outproj_final" if final else "outproj",
    )(*args)


RG_PREV = 2
RG_NEXT = 1


def _softplus(x):
    return jnp.maximum(x, 0.0) + jnp.log(1.0 + jnp.exp(-jnp.abs(x)))


def _rglru_kernel(tc, xf_ref, xfp_ref, xfn_ref, xb_ref, xbp_ref, xbn_ref, cw_ref, cb_ref,
                  wf_ref, bf_ref, wb_ref, bb_ref, lam_ref, h0f_ref, h0b_ref,
                  hf_ref, hb_ref, af_s, bf_s, ab_s, bb_s, cf_s, cbk_s):
    j = pl.program_id(0)
    nj = pl.num_programs(0)
    rows = tc * NB
    ns = xf_ref.shape[0]
    dr = ns * LANES
    wide = lambda ref: _lane_cat([ref[s_] for s_ in range(ns)])

    @pl.when(j == 0)
    def _():
        cf_s[...] = h0f_ref[...]
        cbk_s[...] = h0b_ref[...]

    def coeffs(x_ref, xp_ref, xn_ref, first, last, w_ref, b_ref, lam, a_s, b_s):
        prev = jnp.where(first, 0.0, wide(xp_ref))
        nxt = jnp.where(last, 0.0, wide(xn_ref))
        xfull = jnp.concatenate([prev, wide(x_ref), nxt], axis=0)
        xc = cb_ref[...] + cw_ref[0:1, :] * xfull[0:rows]
        for k in range(1, RG_PREV + RG_NEXT + 1):
            xc = xc + cw_ref[k:k + 1, :] * xfull[k * NB:k * NB + rows]
        g = jnp.dot(xc.astype(BF16), w_ref[...], preferred_element_type=F32) + b_ref[...]
        r = jax.nn.sigmoid(g[:, :dr])
        i = jax.nn.sigmoid(g[:, dr:])
        a = jnp.exp((-RG_C) * r * _softplus(-lam))
        a_s[...] = a
        b_s[...] = jnp.sqrt(1.0 - a * a) * (i * xc)

    coeffs(xf_ref, xfp_ref, xfn_ref, j == 0, j == nj - 1, wf_ref, bf_ref, lam_ref[0:1, :], af_s, bf_s)
    coeffs(xb_ref, xbp_ref, xbn_ref, j == nj - 1, j == 0, wb_ref, bb_ref, lam_ref[1:2, :], ab_s, bb_s)

    def step(t, carry):
        hf, hb = carry
        rf = pl.multiple_of(t * NB, NB)
        rb = pl.multiple_of((tc - 1 - t) * NB, NB)
        hf = af_s[pl.ds(rf, NB), :] * hf + bf_s[pl.ds(rf, NB), :]
        hb = ab_s[pl.ds(rb, NB), :] * hb + bb_s[pl.ds(rb, NB), :]
        for s_ in range(ns):
            hf_ref[s_, pl.ds(rf, NB), :] = _lane_part(hf, s_)
            hb_ref[s_, pl.ds(rb, NB), :] = _lane_part(hb, s_)
        return hf, hb

    hf, hb = lax.fori_loop(0, tc, step, (cf_s[...], cbk_s[...]), unroll=8)
    cf_s[...] = hf
    cbk_s[...] = hb


def _rglru(xr2, conv_w, conv_b, w_f, b_f, w_b, b_b, lam, h0_f, h0_b):
    ns, rows_total, _ = xr2.shape
    dr = ns * LANES
    l = rows_total // NB
    tc = min(l, 128)
    nj = l // tc
    rows = tc * NB
    prev_rows = RG_PREV * NB
    last_prev = rows // prev_rows
    n_next = rows_total // NB

    def blk(shape, fn):
        return pl.BlockSpec(shape, fn)

    full = lambda shape: pl.BlockSpec(shape, lambda j: tuple(0 for _ in shape))
    in_specs = [
        blk((ns, rows, LANES), lambda j: (0, j, 0)),
        blk((ns, prev_rows, LANES), lambda j: (0, jnp.maximum(j * last_prev - 1, 0), 0)),
        blk((ns, NB, LANES), lambda j: (0, jnp.minimum((j + 1) * tc, n_next - 1), 0)),
        blk((ns, rows, LANES), lambda j: (0, nj - 1 - j, 0)),
        blk((ns, prev_rows, LANES), lambda j: (0, jnp.maximum((nj - 1 - j) * last_prev - 1, 0), 0)),
        blk((ns, NB, LANES), lambda j: (0, jnp.minimum((nj - j) * tc, n_next - 1), 0)),
        full(conv_w.shape), full((1, dr)),
        full(w_f.shape), full((1, 2 * dr)), full(w_b.shape), full((1, 2 * dr)),
        full(lam.shape), full((NB, dr)), full((NB, dr)),
    ]
    out_specs = [blk((ns, rows, LANES), lambda j: (0, j, 0)), blk((ns, rows, LANES), lambda j: (0, nj - 1 - j, 0))]
    out_shape = [jax.ShapeDtypeStruct((ns, rows_total, LANES), F32)] * 2
    scratch = [pltpu.VMEM((rows, dr), F32)] * 4 + [pltpu.VMEM((NB, dr), F32)] * 2
    return pl.pallas_call(
        functools.partial(_rglru_kernel, tc),
        grid=(nj,),
        in_specs=in_specs,
        out_specs=out_specs,
        out_shape=out_shape,
        scratch_shapes=scratch,
        compiler_params=_params("arbitrary"),
        name="rglru",
    )(xr2, xr2, xr2, xr2, xr2, xr2, conv_w, conv_b.reshape(1, dr), w_f, b_f, w_b, b_b, lam, h0_f, h0_b)


def _block_diag(w):
    heads, blk, _ = w.shape
    eye = jnp.eye(heads, dtype=w.dtype)
    return jnp.einsum('hij,hg->higj', w, eye).reshape(heads * blk, heads * blk)


def _rglru_weights(wa, ba, wx, bx):
    out = []
    for d in range(2):
        w = jnp.concatenate([_block_diag(wa[d]), _block_diag(wx[d])], axis=1).astype(BF16)
        b = jnp.concatenate([ba[d], bx[d]])[None, :]
        out += [w, b]
    return out


TW_B = 8
ROW_PAD = 8
FFT_COLS = 16
FFT_SLABS = 16


def _fft_dims(l):
    n = 2 * l
    n1 = 16
    while (2 * n1) * (2 * n1) <= n:
        n1 *= 2
    return n1, n // n1


def _pitched_rows(rows, n2):
    return rows // n2 * (n2 + ROW_PAD)


def _store_pitched(store, val, n2):
    p = n2 + ROW_PAD
    pad = jnp.zeros((ROW_PAD, val.shape[1]), val.dtype)
    for g in range(val.shape[0] // n2):
        store(g * p, n2, val[g * n2:(g + 1) * n2])
        store(g * p + n2, ROW_PAD, pad)


def _load_pitched(load, rows, n2):
    p = n2 + ROW_PAD
    return jnp.concatenate([load(g * p, n2) for g in range(rows // n2)], axis=0)


def _slab_pitch(n2):
    return 2 * n2 + ROW_PAD


def _fft_consts(l):
    n1, n2 = _fft_dims(l)
    n = n1 * n2
    h = n1 // 2
    f1 = np.exp(-2j * np.pi * np.outer(np.arange(n1), np.arange(n1)) / n1)
    f2 = np.exp(-2j * np.pi * np.outer(np.arange(n2), np.arange(n2)) / n2)
    stack = lambda a: np.block([[a.real, -a.imag], [a.imag, a.real]])
    bf = lambda a: jnp.asarray(a, F32).astype(BF16)

    def split3(a):
        a32 = jnp.asarray(a, F32)
        hi = a32.astype(BF16)
        lo = (a32 - hi.astype(F32)).astype(BF16)
        return jnp.concatenate([hi, hi, lo], axis=1)

    g, r = np.divmod(np.arange(2 * n2), 16)
    perm = np.where(r < 8, 8 * g + r, n2 + 8 * g + (r - 8))
    inter = lambda a: a[perm][:, perm]

    m1_full = stack(f1)
    m1_half = np.concatenate([m1_full[:, :h], m1_full[:, n1:n1 + h]], axis=1)
    m3_full = stack(np.conj(f1))
    m3 = np.concatenate([m3_full[:h], m3_full[n1:n1 + h]], axis=0)
    na = n1 // TW_B
    ang_a = -2.0 * np.pi * np.outer(np.arange(na) * TW_B, np.arange(n2)) / n
    ang_b = -2.0 * np.pi * np.outer(np.arange(TW_B), np.arange(n2)) / n
    ta = np.concatenate([np.cos(ang_a), np.sin(ang_a)], axis=1)[:, perm]
    ta = jnp.asarray(np.broadcast_to(ta[..., None], ta.shape + (LANES,)), F32)
    tw_b = np.exp(1j * ang_b)
    fwd = [inter(stack(f2 * tw_b[b][None, :])) for b in range(TW_B)]
    inv = [inter(stack(np.conj(f2) * np.conj(tw_b[b])[:, None])) for b in range(TW_B)]
    return dict(n1=n1, n2=n2, m1=bf(m1_half), m3=bf(m3), ta=ta,
                m2b=jnp.stack([bf(m) for m in fwd]), m2ib=jnp.stack([bf(m) for m in inv]),
                m1_real3=split3(m1_full[:, :n1]), m2b3=jnp.stack([split3(m) for m in fwd]))


def _mm(m, x):
    xh = x.astype(BF16)
    if m.shape[1] == 3 * x.shape[0]:
        xl = (x - xh.astype(F32)).astype(BF16)
        xh = jnp.concatenate([xh, xl, xh], axis=0)
    return jnp.dot(m, xh, preferred_element_type=F32)


def _lane_cat(parts):
    return parts[0] if len(parts) == 1 else jnp.concatenate(parts, axis=1)


def _lane_part(x, i):
    return x[:, i * LANES:(i + 1) * LANES]


def _split_ri(x):
    x4 = x.reshape(x.shape[0] // 16, 2, 8, x.shape[1])
    return x4[:, 0], x4[:, 1]


def _join_ri(re, im):
    return jnp.stack([re, im], axis=1).reshape(2 * re.shape[0] * 8, re.shape[2])


def _cmul(x, t, conj=False):
    xr, xi = _split_ri(x)
    tr, ti = _split_ri(t)
    if conj:
        return _join_ri(xr * tr + xi * ti, xi * tr - xr * ti)
    return _join_ri(xr * tr - xi * ti, xr * ti + xi * tr)


def _col_rows(c):
    return (c // 8) * 16 + c % 8


def _stage1(n_in, n1, n2, m_ref, load, a_ref):
    cols = min(FFT_COLS, n2)
    sp = _slab_pitch(n2)

    def body(i, carry):
        out = _mm(m_ref[...], _lane_cat([load(i * cols + q) for q in range(cols)]))
        for q in range(cols):
            r = _col_rows(i * cols + q)
            a_ref[pl.ds(r, n1, stride=sp), :] = _lane_part(out[:n1], q)
            a_ref[pl.ds(r + 8, n1, stride=sp), :] = _lane_part(out[n1:], q)
        return carry

    lax.fori_loop(0, n2 // cols, body, 0)


def _slab_rows(n2, a, b):
    return pl.multiple_of((a * TW_B + b) * _slab_pitch(n2), 8)


def _fftconv_kernel(n1, n2, u_ref, ks_ref, m1_ref, m2b_ref, m2ib_ref, m3_ref, ta_ref, y_ref, a_ref):
    h = n1 // 2
    pitch = n2 + ROW_PAD
    sp = _slab_pitch(n2)
    na = n1 // TW_B
    slabs = min(FFT_SLABS, na)
    groups = na // slabs
    cols = min(FFT_COLS, n2)

    def load_u(c):
        return jnp.concatenate([u_ref[0, pl.ds(c, h, stride=pitch), :],
                                u_ref[1, pl.ds(c, h, stride=pitch), :]], axis=0)

    _stage1(h, n1, n2, m1_ref, load_u, a_ref)

    def slab_stage(i, carry):
        b = i // groups
        a0 = (i % groups) * slabs
        rows = [_slab_rows(n2, a0 + q, b) for q in range(slabs)]
        xs = [_cmul(a_ref[pl.ds(rows[q], 2 * n2), :], ta_ref[a0 + q]) for q in range(slabs)]
        f = _mm(m2b_ref[b], _lane_cat(xs))
        cs = [_cmul(_lane_part(f, q), ks_ref[pl.ds(rows[q], 2 * n2), :]) for q in range(slabs)]
        d = _mm(m2ib_ref[b], _lane_cat(cs))
        for q in range(slabs):
            a_ref[pl.ds(rows[q], 2 * n2), :] = _cmul(_lane_part(d, q), ta_ref[a0 + q], conj=True)
        return carry

    lax.fori_loop(0, TW_B * groups, slab_stage, 0)

    def stage3(i, carry):
        xs = []
        for q in range(cols):
            r = _col_rows(i * cols + q)
            xs.append(jnp.concatenate([a_ref[pl.ds(r, n1, stride=sp), :],
                                       a_ref[pl.ds(r + 8, n1, stride=sp), :]], axis=0))
        out = _mm(m3_ref[...], _lane_cat(xs))
        for q in range(cols):
            c = i * cols + q
            y_ref[0, pl.ds(c, h, stride=pitch), :] = _lane_part(out[:h], q)
            y_ref[1, pl.ds(c, h, stride=pitch), :] = _lane_part(out[h:], q)
        return carry

    lax.fori_loop(0, n2 // cols, stage3, 0)


def _const_spec(a):
    return pl.BlockSpec(a.shape, lambda *_: tuple(0 for _ in a.shape), pipeline_mode=pl.Buffered(1))


def _fftconv(u, kspec, fc):
    b, lp, dh = u.shape
    n1, n2 = fc["n1"], fc["n2"]
    nslab = n1 * _slab_pitch(n2)
    consts = [fc["m1"], fc["m2b"], fc["m2ib"], fc["m3"], fc["ta"]]
    return pl.pallas_call(
        functools.partial(_fftconv_kernel, n1, n2),
        grid=(dh // LANES, b // 2),
        in_specs=[
            pl.BlockSpec((2, lp, LANES), lambda c, i: (i, 0, c), pipeline_mode=pl.Buffered(1)),
            pl.BlockSpec((nslab, LANES), lambda c, i: (0, c), pipeline_mode=pl.Buffered(1)),
        ] + [_const_spec(a) for a in consts],
        out_specs=pl.BlockSpec((2, lp, LANES), lambda c, i: (i, 0, c), pipeline_mode=pl.Buffered(1)),
        out_shape=jax.ShapeDtypeStruct((b, lp, dh), F32),
        scratch_shapes=[pltpu.VMEM((nslab, LANES), F32)],
        compiler_params=_params("parallel", "arbitrary"),
        name="fftconv",
    )(u, kspec, *consts)


def _filter_fft_kernel(n1, n2, k_ref, l1_ref, m1_ref, m2b_ref, ta_ref, o_ref):
    scale = (1.0 / (n1 * n2)) / l1_ref[...]
    pitch = n2 + ROW_PAD
    na = n1 // TW_B
    slabs = min(FFT_SLABS, na)
    groups = na // slabs

    _stage1(n1, n1, n2, m1_ref, lambda c: k_ref[pl.ds(c, n1, stride=pitch), :], o_ref)

    def slab_stage(i, carry):
        b = i // groups
        a0 = (i % groups) * slabs
        rows = [_slab_rows(n2, a0 + q, b) for q in range(slabs)]
        xs = [_cmul(o_ref[pl.ds(rows[q], 2 * n2), :], ta_ref[a0 + q]) for q in range(slabs)]
        f = _mm(m2b_ref[b], _lane_cat(xs))
        for q in range(slabs):
            o_ref[pl.ds(rows[q], 2 * n2), :] = _lane_part(f, q) * scale
            o_ref[pl.ds(rows[q] + 2 * n2, ROW_PAD), :] = jnp.zeros((ROW_PAD, LANES), F32)
        return carry

    lax.fori_loop(0, TW_B * groups, slab_stage, 0)


def _filter_fft(kern, l1, fc):
    npad, dh = kern.shape
    n1, n2 = fc["n1"], fc["n2"]
    nslab = n1 * _slab_pitch(n2)
    consts = [fc["m1_real3"], fc["m2b3"], fc["ta"]]
    return pl.pallas_call(
        functools.partial(_filter_fft_kernel, n1, n2),
        grid=(dh // LANES,),
        in_specs=[pl.BlockSpec((npad, LANES), lambda c: (0, c), pipeline_mode=pl.Buffered(1)),
                  pl.BlockSpec((1, LANES), lambda c: (0, c))]
        + [_const_spec(a) for a in consts],
        out_specs=pl.BlockSpec((nslab, LANES), lambda c: (0, c)),
        out_shape=jax.ShapeDtypeStruct((nslab, dh), F32),
        compiler_params=_params("parallel"),
        name="filter_fft",
    )(kern, l1, *consts)


def _filter_mlp_kernel(z_ref, w1_ref, b1_ref, w2_ref, b2_ref, w3_ref, b3_ref, fr_ref, h_ref):
    dot = lambda a, b: jnp.dot(a, b, precision=HIGHEST, preferred_element_type=F32)
    h = jnp.sin(fr_ref[0:1, :] * (dot(z_ref[...], w1_ref[...]) + b1_ref[...]))
    h = jnp.sin(fr_ref[1:2, :] * (dot(h, w2_ref[...]) + b2_ref[...]))
    h_ref[...] = jnp.sin(fr_ref[2:3, :] * (dot(h, w3_ref[...]) + b3_ref[...]))


def _filter_out_kernel(l, tr, n2, h_ref, w4f_ref, w4b_ref, dl_ref, o_ref, l1_ref):
    rc = pl.program_id(1)
    hcat = h_ref[...]
    hf = jnp.dot(hcat, w4f_ref[...], precision=HIGHEST, preferred_element_type=F32)
    hb = jnp.dot(hcat, w4b_ref[...], precision=HIGHEST, preferred_element_type=F32)
    row = lax.broadcasted_iota(jnp.int32, (tr, LANES), 0) + rc * tr
    step = 1.0 / (l - 1)
    t_f = row.astype(F32) * step
    t_b = (l - row).astype(F32) * step
    kf = hf * jnp.exp(-t_f * dl_ref[...])
    kb = jnp.where(row == 0, 0.0, hb * jnp.exp(-t_b * dl_ref[...]))
    for half, val in ((0, kf), (1, kb)):
        def store(r, n, v, half=half):
            o_ref[half, r:r + n, :] = v
        _store_pitched(store, val, n2)
    part = jnp.sum(jnp.abs(kf), axis=0, keepdims=True) + jnp.sum(jnp.abs(kb), axis=0, keepdims=True)

    @pl.when(rc == 0)
    def _():
        l1_ref[...] = part

    @pl.when(rc > 0)
    def _():
        l1_ref[...] += part


def _filter_feats(l, bands):
    t = jnp.linspace(0.0, 1.0, l, dtype=F32)[:, None]
    w = 2.0 * math.pi * jnp.arange(l, dtype=F32)[:, None] / l
    f = jnp.linspace(1e-4, bands - 1, bands, dtype=F32)[None, :]
    z = jnp.concatenate([t, jnp.cos(f * w), -jnp.sin(f * w)], axis=-1)
    z_rev = jnp.concatenate([z[:1], z[1:][::-1]], axis=0)
    return jnp.concatenate([z, z_rev], axis=-1)


def _hyena_filter(l, n2, f_w1, f_b1, f_w2, f_b2, f_w3, f_b3, f_w4, f_freq):
    emb, hid = f_w1.shape
    dh = f_w4.shape[1] // 2
    tr = min(l, 1024)
    z2 = _filter_feats(l, (emb - 1) // 2)
    bd = lambda w: jnp.kron(jnp.eye(2, dtype=F32), w)
    two = lambda v: jnp.concatenate([v, v], axis=-1)
    margs = [z2, bd(f_w1), two(f_b1)[None], bd(f_w2), two(f_b2)[None], bd(f_w3), two(f_b3)[None], two(f_freq)]
    full1 = lambda a: pl.BlockSpec(a.shape, lambda r: tuple(0 for _ in a.shape))
    hcat = pl.pallas_call(
        _filter_mlp_kernel,
        grid=(l // tr,),
        in_specs=[pl.BlockSpec((tr, 2 * emb), lambda r: (r, 0))] + [full1(a) for a in margs[1:]],
        out_specs=pl.BlockSpec((tr, 2 * hid), lambda r: (r, 0)),
        out_shape=jax.ShapeDtypeStruct((l, 2 * hid), F32),
        compiler_params=_params("parallel"),
        name="filter_mlp",
    )(*margs)
    zeros = jnp.zeros((hid, dh), F32)
    w4f = jnp.concatenate([f_w4[:, :dh], zeros], axis=0)
    w4b = jnp.concatenate([zeros, f_w4[:, dh:]], axis=0)
    max_decay = math.log(FILTER_TARGET) / FAST_DECAY
    min_decay = math.log(FILTER_TARGET) / SLOW_DECAY
    deltas = jnp.abs(jnp.linspace(min_decay, max_decay, dh, dtype=F32))[None, :]
    kern, l1 = pl.pallas_call(
        functools.partial(_filter_out_kernel, l, tr, n2),
        grid=(dh // LANES, l // tr),
        in_specs=[
            pl.BlockSpec((tr, 2 * hid), lambda c, r: (r, 0)),
            pl.BlockSpec((2 * hid, LANES), lambda c, r: (0, c)),
            pl.BlockSpec((2 * hid, LANES), lambda c, r: (0, c)),
            pl.BlockSpec((1, LANES), lambda c, r: (0, c)),
        ],
        out_specs=[pl.BlockSpec((2, _pitched_rows(tr, n2), LANES), lambda c, r: (0, r, c)),
                   pl.BlockSpec((1, LANES), lambda c, r: (0, c))],
        out_shape=[jax.ShapeDtypeStruct((2, _pitched_rows(l, n2), dh), F32), jax.ShapeDtypeStruct((1, dh), F32)],
        compiler_params=_params("parallel", "arbitrary"),
        name="filter_out",
    )(hcat, w4f, w4b, deltas)
    return kern.reshape(2 * _pitched_rows(l, n2), dh), l1


def _grid_sincos(rows, dim):
    r, col = jnp.meshgrid(jnp.arange(rows, dtype=F32), jnp.arange(GRID_W, dtype=F32), indexing='ij')
    quarter = dim // 4
    omega = 1.0 / (10000.0 ** (jnp.arange(quarter, dtype=F32) / quarter))

    def emb(pos):
        ang = pos.reshape(-1, 1) * omega[None, :]
        return jnp.concatenate([jnp.sin(ang), jnp.cos(ang)], axis=-1)

    return jnp.concatenate([emb(r), emb(col)], axis=-1)


def kernel(x, c, ctx, c_ctx, w_mod, b_mod, norm_g, w_in, hy_conv_w, hy_conv_b, f_w1, f_b1, f_w2, f_b2,
           f_w3, f_b3, f_w4, f_freq, hy_bias, rg_conv_w, rg_conv_b, rg_wa, rg_ba, rg_wx, rg_bx, rg_lam,
           br_norm_h, br_norm_r, w_out, final_g):
    B, N, D = x.shape
    L_ctx = ctx.shape[1]
    depth = w_mod.shape[0]
    dh = hy_bias.shape[1]
    dr = rg_conv_b.shape[1]
    assert B == NB and B % BB == 0 and dh == dr

    pos = _grid_sincos(N // GRID_W, D)
    cond = jnp.concatenate([c, jnp.broadcast_to(c_ctx[None], (8, D))], axis=0)
    mod = _modulation(cond, w_mod, b_mod)
    w_in_bf = w_in.astype(BF16)
    w_out_bf = w_out.astype(BF16)
    fc_lat = _fft_consts(N)
    fc_ctx = _fft_consts(L_ctx)

    n2_l, n2_c = fc_lat["n2"], fc_ctx["n2"]
    state = lambda h, rows: h[:, rows, :].transpose(1, 0, 2).reshape(NB, dr)

    xl, xc = x, ctx
    for l in range(depth):
        last = l == depth - 1
        ml = mod[l, :B].reshape(B, 1, 3 * D)
        mc = jnp.broadcast_to(mod[l, B:B + 1].reshape(1, 1, 3 * D), (B, 1, 3 * D))
        sh_l, sc_l, g_l = ml[..., :D], ml[..., D:2 * D], ml[..., 2 * D:]
        sh_c, sc_c, g_c = mc[..., :D], mc[..., D:2 * D], mc[..., 2 * D:]
        proj = (norm_g[l], w_in_bf[l], hy_conv_w[l], hy_conv_b[l], dh, dr)
        if l == 0:
            u_l, x0_l, zh_l, zr_l, xr_l, xl = _inproj(xl, pos, sh_l, sc_l, *proj, n2_l)
        else:
            u_l, x0_l, zh_l, zr_l, xr_l = _inproj(xl, None, sh_l, sc_l, *proj, n2_l)
        u_c, x0_c, zh_c, zr_c, xr_c = _inproj(xc, None, sh_c, sc_c, *proj, n2_c)
        rnn_w = _rglru_weights(rg_wa[l], rg_ba[l], rg_wx[l], rg_bx[l])
        zero = jnp.zeros((NB, dr), F32)
        hf_c, hb_c = _rglru(xr_c, rg_conv_w[l], rg_conv_b[l], *rnn_w, rg_lam[l], zero, zero)
        hf_l, hb_l = _rglru(xr_l, rg_conv_w[l], rg_conv_b[l], *rnn_w, rg_lam[l],
                            state(hf_c, slice(-NB, None)), state(hb_c, slice(0, NB)))
        filt = (f_w1[l], f_b1[l], f_w2[l], f_b2[l], f_w3[l], f_b3[l], f_w4[l], f_freq[l])
        shared = (br_norm_h[l], br_norm_r[l], w_out_bf[l])
        cv_l = _fftconv(u_l, _filter_fft(*_hyena_filter(N, n2_l, *filt), fc_lat), fc_lat)
        if not last:
            cv_c = _fftconv(u_c, _filter_fft(*_hyena_filter(L_ctx, n2_c, *filt), fc_ctx), fc_ctx)
            xc = _outproj(cv_c, u_c, x0_c, hy_bias[l], n2_c, hf_c, hb_c, zh_c, zr_c, *shared, xc, g_c, None)
        xl = _outproj(cv_l, u_l, x0_l, hy_bias[l], n2_l, hf_l, hb_l, zh_l, zr_l, *shared, xl, g_l,
                      final_g if last else None)
    return xl
```

```python
import functools
import math

import jax
import jax.numpy as jnp
import numpy as np
from jax import lax
from jax.experimental import pallas as pl
from jax.experimental.pallas import tpu as pltpu

GRID_W = 64
FILTER_TARGET = 1e-2
FAST_DECAY = 0.3
SLOW_DECAY = 1.5
RG_C = 8.0
EPS = 1e-6

LANES = 128
F32 = jnp.float32
BF16 = jnp.bfloat16
HIGHEST = lax.Precision.HIGHEST

VMEM_LIMIT_BYTES = 56 * 1024 * 1024


def _params(*sem):
    return pltpu.CompilerParams(dimension_semantics=sem, vmem_limit_bytes=VMEM_LIMIT_BYTES)


def _silu(x):
    return x * jax.nn.sigmoid(x)


def _rms(x, g):
    return x * lax.rsqrt(jnp.mean(x * x, axis=-1, keepdims=True) + EPS) * g


def _mod_kernel(s_ref, w_ref, b_ref, o_ref):
    s = _silu(s_ref[...])
    o_ref[0] = jnp.dot(s, w_ref[0], precision=HIGHEST, preferred_element_type=F32) + b_ref[0]


def _modulation(cond, w_mod, b_mod):
    depth, d, d3 = w_mod.shape
    rows = cond.shape[0]
    tn = d // 2
    return pl.pallas_call(
        _mod_kernel,
        grid=(depth, d3 // tn),
        in_specs=[
            pl.BlockSpec((rows, d), lambda l, j: (0, 0)),
            pl.BlockSpec((1, d, tn), lambda l, j: (l, 0, j)),
            pl.BlockSpec((1, 1, tn), lambda l, j: (l, 0, j)),
        ],
        out_specs=pl.BlockSpec((1, rows, tn), lambda l, j: (l, 0, j)),
        out_shape=jax.ShapeDtypeStruct((depth, rows, d3), F32),
        compiler_params=_params("parallel", "parallel"),
        name="modulation",
    )(cond, w_mod, b_mod.reshape(depth, 1, d3))


NB = 8
HALO = 8
BB = 4


def _inproj_kernel(add_pos, tm, dh, dr, n2, *refs):
    refs = list(refs)
    x_ref, xp_ref, xn_ref = refs[:3]
    refs = refs[3:]
    if add_pos:
        pos_ref, posp_ref, posn_ref = refs[:3]
        refs = refs[3:]
    (sh_ref, sc_ref, g_ref, wh_ref, wr_ref, cw_ref, cb_ref, u_ref, x0_ref, zh_ref, zr_ref, xr_ref) = refs[:12]
    j = pl.program_id(0)
    i = pl.program_id(1)
    keep_prev = (j > 0).astype(F32)
    keep_next = (j < pl.num_programs(0) - 1).astype(F32)
    seg = tm + 2 * HALO

    xs = []
    for b in range(BB):
        xb = jnp.concatenate([xp_ref[b], x_ref[b], xn_ref[b]], axis=0)
        if add_pos:
            xb = xb + jnp.concatenate([posp_ref[...], pos_ref[...], posn_ref[...]], axis=0)
            refs[12][b] = xb[HALO:HALO + tm]
        xs.append((_rms(xb, g_ref[...]) * (1.0 + sc_ref[b]) + sh_ref[b]).astype(BF16))
    xn_all = jnp.concatenate(xs, axis=0)

    ph = jnp.dot(xn_all, wh_ref[...], preferred_element_type=F32)
    for b in range(BB):
        pb = ph[b * seg:(b + 1) * seg]
        full = jnp.concatenate([pb[:HALO] * keep_prev, pb[HALO:HALO + tm], pb[HALO + tm:] * keep_next], axis=0)
        c = cb_ref[...] + cw_ref[0:1, :] * full[HALO - 1:HALO - 1 + tm]
        c = c + cw_ref[1:2, :] * full[HALO:HALO + tm]
        c = c + cw_ref[2:3, :] * full[HALO + 1:HALO + 1 + tm]
        x0_ref[b] = c[:, :dh].astype(x0_ref.dtype)

        def store(r, n, v, b=b):
            u_ref[b, r:r + n, :] = v
        _store_pitched(store, c[:, 2 * dh:] * c[:, dh:2 * dh], n2)

    xn_mid = jnp.concatenate([x[HALO:HALO + tm] for x in xs], axis=0)
    pr = jnp.dot(xn_mid, wr_ref[...], preferred_element_type=F32)
    for b in range(BB):
        pb = pr[b * tm:(b + 1) * tm]
        zh_ref[b] = pb[:, :dh].astype(zh_ref.dtype)
        zr_ref[b] = pb[:, dh + dr:].astype(zr_ref.dtype)
        for s_ in range(dr // LANES):
            xr_ref[s_, pl.ds(i * BB + b, tm, stride=NB), :] = pb[:, dh + s_ * LANES:dh + (s_ + 1) * LANES]


def _inproj(x, pos, sh, sc, g, w_bf16, conv_w, conv_b, dh, dr, n2):
    b, l, d = x.shape
    tm = min(l, 128)
    add_pos = pos is not None
    nt = l // HALO
    per = tm // HALO
    row = lambda w: pl.BlockSpec((BB, tm, w), lambda j, i: (i, j, 0))
    prev = lambda j: jnp.maximum(j * per - 1, 0)
    nxt = lambda j: jnp.minimum((j + 1) * per, nt - 1)
    vec = pl.BlockSpec((BB, 1, d), lambda j, i: (i, 0, 0))
    full = lambda a: pl.BlockSpec(a.shape, lambda j, i: tuple(0 for _ in a.shape), pipeline_mode=pl.Buffered(1))
    in_specs = [row(d),
                pl.BlockSpec((BB, HALO, d), lambda j, i: (i, prev(j), 0)),
                pl.BlockSpec((BB, HALO, d), lambda j, i: (i, nxt(j), 0))]
    args = [x, x, x]
    if add_pos:
        in_specs += [pl.BlockSpec((tm, d), lambda j, i: (j, 0)),
                     pl.BlockSpec((HALO, d), lambda j, i: (prev(j), 0)),
                     pl.BlockSpec((HALO, d), lambda j, i: (nxt(j), 0))]
        args += [pos, pos, pos]
    w_h = w_bf16[:, :3 * dh]
    w_r = w_bf16[:, 3 * dh:]
    consts = [g.reshape(1, d), w_h, w_r, conv_w, conv_b.reshape(1, 3 * dh)]
    in_specs += [vec, vec] + [full(a) for a in consts]
    args += [sh, sc] + consts
    out_specs = [pl.BlockSpec((BB, _pitched_rows(tm, n2), dh), lambda j, i: (i, j, 0)),
                 row(dh), row(dh), row(dr),
                 pl.BlockSpec((dr // LANES, tm * NB, LANES), lambda j, i: (0, j, 0))]
    out_shape = [jax.ShapeDtypeStruct((b, _pitched_rows(l, n2), dh), F32),
                 jax.ShapeDtypeStruct((b, l, dh), BF16), jax.ShapeDtypeStruct((b, l, dh), BF16),
                 jax.ShapeDtypeStruct((b, l, dr), BF16),
                 jax.ShapeDtypeStruct((dr // LANES, l * NB, LANES), F32)]
    if add_pos:
        out_specs.append(row(d))
        out_shape.append(jax.ShapeDtypeStruct((b, l, d), F32))
    return pl.pallas_call(
        functools.partial(_inproj_kernel, add_pos, tm, dh, dr, n2),
        grid=(l // tm, b // BB),
        in_specs=in_specs,
        out_specs=out_specs,
        out_shape=out_shape,
        compiler_params=_params("arbitrary", "arbitrary"),
        name="inproj_pos" if add_pos else "inproj",
    )(*args)


def _outproj_kernel(final, tm, n2, *refs):
    if final:
        (cv_ref, x0_ref, hf_ref, hb_ref, zh_ref, zr_ref, gnh_ref, gnr_ref, wt_ref, wb_ref,
         x_ref, gate_ref, fg_ref, o_ref) = refs
    else:
        (cv_ref, x0_ref, hf_ref, hb_ref, zh_ref, zr_ref, gnh_ref, gnr_ref, wt_ref, wb_ref,
         x_ref, gate_ref, o_ref) = refs
    i = pl.program_id(1)
    yhs, yrs = [], []
    for b in range(BB):
        cv = _load_pitched(lambda r, n, b=b: cv_ref[b, r:r + n, :], tm, n2)
        y_h = cv * x0_ref[b].astype(F32)
        yhs.append((_rms(y_h, gnh_ref[...]) * _silu(zh_ref[b].astype(F32))).astype(BF16))
        hs = _lane_cat([hf_ref[s_, pl.ds(i * BB + b, tm, stride=NB), :] + hb_ref[s_, pl.ds(i * BB + b, tm, stride=NB), :]
                        for s_ in range(hf_ref.shape[0])])
        yrs.append((_rms(hs, gnr_ref[...]) * _silu(zr_ref[b].astype(F32))).astype(BF16))
    y = jnp.dot(jnp.concatenate(yhs, axis=0), wt_ref[...], preferred_element_type=F32)
    y = y + jnp.dot(jnp.concatenate(yrs, axis=0), wb_ref[...], preferred_element_type=F32)
    for b in range(BB):
        out = x_ref[b] + gate_ref[b] * y[b * tm:(b + 1) * tm]
        if final:
            out = _rms(out, fg_ref[...])
        o_ref[b] = out


def _outproj(conv, x0, n2, hf, hb, zh, zr, gn_h, gn_r, w_out_bf16, x, gate, final_g):
    b, l, d = x.shape
    dh = conv.shape[-1]
    dr = zr.shape[-1]
    tm = min(l, 128)
    final = final_g is not None
    row = lambda w: pl.BlockSpec((BB, tm, w), lambda j, i: (i, j, 0))
    pitched = pl.BlockSpec((BB, _pitched_rows(tm, n2), dh), lambda j, i: (i, j, 0))
    slabs = pl.BlockSpec((dr // LANES, tm * NB, LANES), lambda j, i: (0, j, 0))
    full = lambda a: pl.BlockSpec(a.shape, lambda j, i: tuple(0 for _ in a.shape), pipeline_mode=pl.Buffered(1))
    consts = [gn_h.reshape(1, dh), gn_r.reshape(1, dr), w_out_bf16[:dh], w_out_bf16[dh:]]
    in_specs = [pitched, row(dh), slabs, slabs, row(dh), row(dr)] + [full(a) for a in consts] + [
        row(d), pl.BlockSpec((BB, 1, d), lambda j, i: (i, 0, 0))]
    args = [conv, x0, hf, hb, zh, zr] + consts + [x, gate]
    if final:
        fg = final_g.reshape(1, d)
        in_specs.append(full(fg))
        args.append(fg)
    return pl.pallas_call(
        functools.partial(_outproj_kernel, final, tm, n2),
        grid=(l // tm, b // BB),
        in_specs=in_specs,
        out_specs=row(d),
        out_shape=jax.ShapeDtypeStruct((b, l, d), F32),
        compiler_params=_params("parallel", "parallel"),
        name="outproj_final" if final else "outproj",
    )(*args)


RG_PREV = 2
RG_NEXT = 1
SQRT_FLOOR = 1e-30
LOG2E = math.log2(math.e)


def _softplus(x):
    return jnp.maximum(x, 0.0) + jnp.log(1.0 + jnp.exp(-jnp.abs(x)))


def _sqrt01(y):
    return y * lax.rsqrt(jnp.maximum(y, SQRT_FLOOR))


def _rglru_kernel(tc, xf_ref, xfp_ref, xfn_ref, xb_ref, xbp_ref, xbn_ref, cw_ref, cb_ref,
                  wf_ref, bf_ref, wb_ref, bb_ref, lam_ref, h0f_ref, h0b_ref,
                  hf_ref, hb_ref, af_s, bf_s, ab_s, bb_s, cf_s, cbk_s):
    j = pl.program_id(0)
    nj = pl.num_programs(0)
    rows = tc * NB
    ns = xf_ref.shape[0]
    dr = ns * LANES
    wide = lambda ref: _lane_cat([ref[s_] for s_ in range(ns)])

    @pl.when(j == 0)
    def _():
        cf_s[...] = h0f_ref[...]
        cbk_s[...] = h0b_ref[...]

    def coeffs(x_ref, xp_ref, xn_ref, first, last, w_ref, b_ref, lam, a_s, b_s):
        prev = jnp.where(first, 0.0, wide(xp_ref))
        nxt = jnp.where(last, 0.0, wide(xn_ref))
        xfull = jnp.concatenate([prev, wide(x_ref), nxt], axis=0)
        xh = cb_ref[...] + cw_ref[0:1, :] * xfull[0:rows]
        for k in range(1, RG_PREV + RG_NEXT + 1):
            xh = xh + cw_ref[k:k + 1, :] * xfull[k * NB:k * NB + rows]
        g = jnp.dot(xh.astype(BF16), w_ref[...], preferred_element_type=F32) + b_ref[...]
        t_r = jnp.tanh(g[:, :dr])
        t_i = jnp.tanh(g[:, dr:])
        c1 = (-0.5 * RG_C * LOG2E) * _softplus(-lam)
        a = jnp.exp2(c1 * t_r + c1)
        a_s[...] = a
        b_s[...] = _sqrt01(1.0 - a * a) * (t_i * xh + xh)

    coeffs(xf_ref, xfp_ref, xfn_ref, j == 0, j == nj - 1, wf_ref, bf_ref, lam_ref[0:1, :], af_s, bf_s)
    coeffs(xb_ref, xbp_ref, xbn_ref, j == nj - 1, j == 0, wb_ref, bb_ref, lam_ref[1:2, :], ab_s, bb_s)

    def step(t, carry):
        hf, hb = carry
        rf = pl.multiple_of(t * NB, NB)
        rb = pl.multiple_of((tc - 1 - t) * NB, NB)
        hf = af_s[pl.ds(rf, NB), :] * hf + bf_s[pl.ds(rf, NB), :]
        hb = ab_s[pl.ds(rb, NB), :] * hb + bb_s[pl.ds(rb, NB), :]
        for s_ in range(ns):
            hf_ref[s_, pl.ds(rf, NB), :] = _lane_part(hf, s_)
            hb_ref[s_, pl.ds(rb, NB), :] = _lane_part(hb, s_)
        return hf, hb

    hf, hb = lax.fori_loop(0, tc, step, (cf_s[...], cbk_s[...]), unroll=8)
    cf_s[...] = hf
    cbk_s[...] = hb


def _rglru(xr2, conv_w, conv_b, w_f, b_f, w_b, b_b, lam, h0_f, h0_b):
    ns, rows_total, _ = xr2.shape
    dr = ns * LANES
    l = rows_total // NB
    tc = min(l, 128)
    nj = l // tc
    rows = tc * NB
    prev_rows = RG_PREV * NB
    last_prev = rows // prev_rows
    n_next = rows_total // NB

    def blk(shape, fn):
        return pl.BlockSpec(shape, fn)

    full = lambda shape: pl.BlockSpec(shape, lambda j: tuple(0 for _ in shape))
    in_specs = [
        blk((ns, rows, LANES), lambda j: (0, j, 0)),
        blk((ns, prev_rows, LANES), lambda j: (0, jnp.maximum(j * last_prev - 1, 0), 0)),
        blk((ns, NB, LANES), lambda j: (0, jnp.minimum((j + 1) * tc, n_next - 1), 0)),
        blk((ns, rows, LANES), lambda j: (0, nj - 1 - j, 0)),
        blk((ns, prev_rows, LANES), lambda j: (0, jnp.maximum((nj - 1 - j) * last_prev - 1, 0), 0)),
        blk((ns, NB, LANES), lambda j: (0, jnp.minimum((nj - j) * tc, n_next - 1), 0)),
        full(conv_w.shape), full((1, dr)),
        full(w_f.shape), full((1, 2 * dr)), full(w_b.shape), full((1, 2 * dr)),
        full(lam.shape), full((NB, dr)), full((NB, dr)),
    ]
    out_specs = [blk((ns, rows, LANES), lambda j: (0, j, 0)), blk((ns, rows, LANES), lambda j: (0, nj - 1 - j, 0))]
    out_shape = [jax.ShapeDtypeStruct((ns, rows_total, LANES), F32)] * 2
    scratch = [pltpu.VMEM((rows, dr), F32)] * 4 + [pltpu.VMEM((NB, dr), F32)] * 2
    return pl.pallas_call(
        functools.partial(_rglru_kernel, tc),
        grid=(nj,),
        in_specs=in_specs,
        out_specs=out_specs,
        out_shape=out_shape,
        scratch_shapes=scratch,
        compiler_params=_params("arbitrary"),
        name="rglru",
    )(xr2, xr2, xr2, xr2, xr2, xr2, 0.5 * conv_w, 0.5 * conv_b.reshape(1, dr), w_f, 0.5 * b_f, w_b, 0.5 * b_b,
      lam, h0_f, h0_b)


def _block_diag(w):
    heads, blk, _ = w.shape
    eye = jnp.eye(heads, dtype=w.dtype)
    return jnp.einsum('hij,hg->higj', w, eye).reshape(heads * blk, heads * blk)


def _rglru_weights(wa, ba, wx, bx):
    out = []
    for d in range(2):
        w = jnp.concatenate([_block_diag(wa[d]), _block_diag(wx[d])], axis=1).astype(BF16)
        b = jnp.concatenate([ba[d], bx[d]])[None, :]
        out += [w, b]
    return out


TW_B = 8
ROW_PAD = 8
FFT_COLS = 16
FFT_SLABS = 16


def _fft_dims(l):
    n = 2 * l
    n1 = 16
    while (2 * n1) * (2 * n1) <= n:
        n1 *= 2
    return n1, n // n1


def _pitched_rows(rows, n2):
    return rows // n2 * (n2 + ROW_PAD)


def _store_pitched(store, val, n2):
    p = n2 + ROW_PAD
    pad = jnp.zeros((ROW_PAD, val.shape[1]), val.dtype)
    for g in range(val.shape[0] // n2):
        store(g * p, n2, val[g * n2:(g + 1) * n2])
        store(g * p + n2, ROW_PAD, pad)


def _load_pitched(load, rows, n2):
    p = n2 + ROW_PAD
    return jnp.concatenate([load(g * p, n2) for g in range(rows // n2)], axis=0)


def _slab_pitch(n2):
    return 2 * n2 + ROW_PAD


def _fft_consts(l):
    n1, n2 = _fft_dims(l)
    n = n1 * n2
    h = n1 // 2
    f1 = np.exp(-2j * np.pi * np.outer(np.arange(n1), np.arange(n1)) / n1)
    f2 = np.exp(-2j * np.pi * np.outer(np.arange(n2), np.arange(n2)) / n2)
    stack = lambda a: np.block([[a.real, -a.imag], [a.imag, a.real]])
    bf = lambda a: jnp.asarray(a, F32).astype(BF16)

    def split3(a):
        a32 = jnp.asarray(a, F32)
        hi = a32.astype(BF16)
        lo = (a32 - hi.astype(F32)).astype(BF16)
        return jnp.concatenate([hi, hi, lo], axis=1)

    g, r = np.divmod(np.arange(2 * n2), 16)
    perm = np.where(r < 8, 8 * g + r, n2 + 8 * g + (r - 8))
    inter = lambda a: a[perm][:, perm]

    m1_full = stack(f1)
    m1_half = np.concatenate([m1_full[:, :h], m1_full[:, n1:n1 + h]], axis=1)
    m3_full = stack(np.conj(f1))
    m3 = np.concatenate([m3_full[:h], m3_full[n1:n1 + h]], axis=0)
    na = n1 // TW_B
    ang_a = -2.0 * np.pi * np.outer(np.arange(na) * TW_B, np.arange(n2)) / n
    ang_b = -2.0 * np.pi * np.outer(np.arange(TW_B), np.arange(n2)) / n
    ta = np.concatenate([np.cos(ang_a), np.sin(ang_a)], axis=1)[:, perm]
    ta = jnp.asarray(np.broadcast_to(ta[..., None], ta.shape + (LANES,)), F32)
    tw_b = np.exp(1j * ang_b)
    fwd = [inter(stack(f2 * tw_b[b][None, :])) for b in range(TW_B)]
    inv = [inter(stack(np.conj(f2) * np.conj(tw_b[b])[:, None])) for b in range(TW_B)]
    return dict(n1=n1, n2=n2, m1=bf(m1_half), m3=bf(m3), ta=ta,
                m2b=jnp.stack([bf(m) for m in fwd]), m2ib=jnp.stack([bf(m) for m in inv]),
                m1_real3=split3(m1_full[:, :n1]), m2b3=jnp.stack([split3(m) for m in fwd]))


def _mm(m, x):
    xh = x.astype(BF16)
    if m.shape[1] == 3 * x.shape[0]:
        xl = (x - xh.astype(F32)).astype(BF16)
        xh = jnp.concatenate([xh, xl, xh], axis=0)
    return jnp.dot(m, xh, preferred_element_type=F32)


def _lane_cat(parts):
    return parts[0] if len(parts) == 1 else jnp.concatenate(parts, axis=1)


def _lane_part(x, i):
    return x[:, i * LANES:(i + 1) * LANES]


def _split_ri(x):
    x4 = x.reshape(x.shape[0] // 16, 2, 8, x.shape[1])
    return x4[:, 0], x4[:, 1]


def _join_ri(re, im):
    return jnp.stack([re, im], axis=1).reshape(2 * re.shape[0] * 8, re.shape[2])


def _cmul(x, t, conj=False):
    xr, xi = _split_ri(x)
    tr, ti = _split_ri(t)
    if conj:
        return _join_ri(xr * tr + xi * ti, xi * tr - xr * ti)
    return _join_ri(xr * tr - xi * ti, xr * ti + xi * tr)


def _col_rows(c):
    return (c // 8) * 16 + c % 8


def _stage1(n_in, n1, n2, m_ref, load, a_ref):
    cols = min(FFT_COLS, n2)
    sp = _slab_pitch(n2)

    def body(i, carry):
        out = _mm(m_ref[...], _lane_cat([load(i * cols + q) for q in range(cols)]))
        for q in range(cols):
            r = _col_rows(i * cols + q)
            a_ref[pl.ds(r, n1, stride=sp), :] = _lane_part(out[:n1], q)
            a_ref[pl.ds(r + 8, n1, stride=sp), :] = _lane_part(out[n1:], q)
        return carry

    lax.fori_loop(0, n2 // cols, body, 0)


def _slab_rows(n2, a, b):
    return pl.multiple_of((a * TW_B + b) * _slab_pitch(n2), 8)


def _fftconv_kernel(n1, n2, u_hbm, ks_ref, m1_ref, m2b_ref, m2ib_ref, m3_ref, ta_ref, y_hbm,
                    a_ref, buf, sem_in, sem_out):
    h = n1 // 2
    pitch = n2 + ROW_PAD
    sp = _slab_pitch(n2)
    na = n1 // TW_B
    slabs = min(FFT_SLABS, na)
    groups = na // slabs
    cols = min(FFT_COLS, n2)
    npairs = pl.num_programs(1)
    total = pl.num_programs(0) * npairs
    t = pl.program_id(0) * npairs + pl.program_id(1)
    slot = t % 2

    def window(ref, tt):
        lane0 = pl.multiple_of((tt // npairs) * LANES, LANES)
        return ref.at[pl.ds(2 * (tt % npairs), 2), :, pl.ds(lane0, LANES)]

    def in_copy(tt, s):
        return pltpu.make_async_copy(window(u_hbm, tt), buf.at[s], sem_in.at[s])

    def out_copy(tt, s):
        return pltpu.make_async_copy(buf.at[s], window(y_hbm, tt), sem_out.at[s])

    @pl.when(t == 0)
    def _():
        in_copy(t, slot).start()

    in_copy(t, slot).wait()
    u_ref = buf.at[slot]

    def load_u(c):
        return jnp.concatenate([u_ref[0, pl.ds(c, h, stride=pitch), :],
                                u_ref[1, pl.ds(c, h, stride=pitch), :]], axis=0)

    _stage1(h, n1, n2, m1_ref, load_u, a_ref)

    @pl.when(t >= 1)
    def _():
        out_copy(t - 1, 1 - slot).wait()

    @pl.when(t + 1 < total)
    def _():
        in_copy(t + 1, 1 - slot).start()

    def slab_stage(i, carry):
        b = i // groups
        a0 = (i % groups) * slabs
        rows = [_slab_rows(n2, a0 + q, b) for q in range(slabs)]
        xs = [_cmul(a_ref[pl.ds(rows[q], 2 * n2), :], ta_ref[a0 + q]) for q in range(slabs)]
        f = _mm(m2b_ref[b], _lane_cat(xs))
        cs = [_cmul(_lane_part(f, q), ks_ref[pl.ds(rows[q], 2 * n2), :]) for q in range(slabs)]
        d = _mm(m2ib_ref[b], _lane_cat(cs))
        for q in range(slabs):
            a_ref[pl.ds(rows[q], 2 * n2), :] = _cmul(_lane_part(d, q), ta_ref[a0 + q], conj=True)
        return carry

    lax.fori_loop(0, TW_B * groups, slab_stage, 0)

    def stage3(i, carry):
        xs = []
        for q in range(cols):
            r = _col_rows(i * cols + q)
            xs.append(jnp.concatenate([a_ref[pl.ds(r, n1, stride=sp), :],
                                       a_ref[pl.ds(r + 8, n1, stride=sp), :]], axis=0))
        out = _mm(m3_ref[...], _lane_cat(xs))
        for q in range(cols):
            c = i * cols + q
            u_ref[0, pl.ds(c, h, stride=pitch), :] = _lane_part(out[:h], q)
            u_ref[1, pl.ds(c, h, stride=pitch), :] = _lane_part(out[h:], q)
        return carry

    lax.fori_loop(0, n2 // cols, stage3, 0)

    out_copy(t, slot).start()

    @pl.when(t == total - 1)
    def _():
        out_copy(t, slot).wait()


def _const_spec(a):
    return pl.BlockSpec(a.shape, lambda *_: tuple(0 for _ in a.shape), pipeline_mode=pl.Buffered(1))


def _fftconv(u, kspec, fc):
    b, lp, dh = u.shape
    n1, n2 = fc["n1"], fc["n2"]
    nslab = n1 * _slab_pitch(n2)
    consts = [fc["m1"], fc["m2b"], fc["m2ib"], fc["m3"], fc["ta"]]
    return pl.pallas_call(
        functools.partial(_fftconv_kernel, n1, n2),
        grid=(dh // LANES, b // 2),
        in_specs=[
            pl.BlockSpec(memory_space=pl.ANY),
            pl.BlockSpec((nslab, LANES), lambda c, i: (0, c), pipeline_mode=pl.Buffered(1)),
        ] + [_const_spec(a) for a in consts],
        out_specs=pl.BlockSpec(memory_space=pl.ANY),
        out_shape=jax.ShapeDtypeStruct((b, lp, dh), F32),
        scratch_shapes=[pltpu.VMEM((nslab, LANES), F32), pltpu.VMEM((2, 2, lp, LANES), F32),
                        pltpu.SemaphoreType.DMA((2,)), pltpu.SemaphoreType.DMA((2,))],
        compiler_params=_params("arbitrary", "arbitrary"),
        name="fftconv",
    )(u, kspec, *consts)


def _filter_fft_kernel(n1, n2, k_ref, l1_ref, hyb_ref, m1_ref, m2b_ref, ta_ref, o_ref):
    scale = (1.0 / (n1 * n2)) / l1_ref[...]
    skip = hyb_ref[...] * (1.0 / (n1 * n2))
    pitch = n2 + ROW_PAD
    na = n1 // TW_B
    slabs = min(FFT_SLABS, na)
    groups = na // slabs

    _stage1(n1, n1, n2, m1_ref, lambda c: k_ref[pl.ds(c, n1, stride=pitch), :], o_ref)

    def slab_stage(i, carry):
        b = i // groups
        a0 = (i % groups) * slabs
        rows = [_slab_rows(n2, a0 + q, b) for q in range(slabs)]
        xs = [_cmul(o_ref[pl.ds(rows[q], 2 * n2), :], ta_ref[a0 + q]) for q in range(slabs)]
        f = _mm(m2b_ref[b], _lane_cat(xs))
        for q in range(slabs):
            fr, fi = _split_ri(_lane_part(f, q) * scale)
            o_ref[pl.ds(rows[q], 2 * n2), :] = _join_ri(fr + skip, fi)
            o_ref[pl.ds(rows[q] + 2 * n2, ROW_PAD), :] = jnp.zeros((ROW_PAD, LANES), F32)
        return carry

    lax.fori_loop(0, TW_B * groups, slab_stage, 0)


def _filter_fft(kern, l1, hy_bias, fc):
    npad, dh = kern.shape
    n1, n2 = fc["n1"], fc["n2"]
    nslab = n1 * _slab_pitch(n2)
    consts = [fc["m1_real3"], fc["m2b3"], fc["ta"]]
    return pl.pallas_call(
        functools.partial(_filter_fft_kernel, n1, n2),
        grid=(dh // LANES,),
        in_specs=[pl.BlockSpec((npad, LANES), lambda c: (0, c), pipeline_mode=pl.Buffered(1)),
                  pl.BlockSpec((1, LANES), lambda c: (0, c)),
                  pl.BlockSpec((1, LANES), lambda c: (0, c))]
        + [_const_spec(a) for a in consts],
        out_specs=pl.BlockSpec((nslab, LANES), lambda c: (0, c)),
        out_shape=jax.ShapeDtypeStruct((nslab, dh), F32),
        compiler_params=_params("parallel"),
        name="filter_fft",
    )(kern, l1, hy_bias.reshape(1, dh), *consts)


def _filter_mlp_kernel(z_ref, w1_ref, b1_ref, w2_ref, b2_ref, w3_ref, b3_ref, fr_ref, h_ref):
    dot = lambda a, b: jnp.dot(a, b, precision=HIGHEST, preferred_element_type=F32)
    h = jnp.sin(fr_ref[0:1, :] * (dot(z_ref[...], w1_ref[...]) + b1_ref[...]))
    h = jnp.sin(fr_ref[1:2, :] * (dot(h, w2_ref[...]) + b2_ref[...]))
    h_ref[...] = jnp.sin(fr_ref[2:3, :] * (dot(h, w3_ref[...]) + b3_ref[...]))


def _filter_out_kernel(l, tr, n2, h_ref, w4f_ref, w4b_ref, dl_ref, o_ref, l1_ref):
    rc = pl.program_id(1)
    hcat = h_ref[...]
    hf = jnp.dot(hcat, w4f_ref[...], precision=HIGHEST, preferred_element_type=F32)
    hb = jnp.dot(hcat, w4b_ref[...], precision=HIGHEST, preferred_element_type=F32)
    row = lax.broadcasted_iota(jnp.int32, (tr, LANES), 0) + rc * tr
    step = 1.0 / (l - 1)
    t_f = row.astype(F32) * step
    t_b = (l - row).astype(F32) * step
    kf = hf * jnp.exp(-t_f * dl_ref[...])
    kb = jnp.where(row == 0, 0.0, hb * jnp.exp(-t_b * dl_ref[...]))
    for half, val in ((0, kf), (1, kb)):
        def store(r, n, v, half=half):
            o_ref[half, r:r + n, :] = v
        _store_pitched(store, val, n2)
    part = jnp.sum(jnp.abs(kf), axis=0, keepdims=True) + jnp.sum(jnp.abs(kb), axis=0, keepdims=True)

    @pl.when(rc == 0)
    def _():
        l1_ref[...] = part

    @pl.when(rc > 0)
    def _():
        l1_ref[...] += part


def _filter_feats(l, bands):
    t = jnp.linspace(0.0, 1.0, l, dtype=F32)[:, None]
    w = 2.0 * math.pi * jnp.arange(l, dtype=F32)[:, None] / l
    f = jnp.linspace(1e-4, bands - 1, bands, dtype=F32)[None, :]
    z = jnp.concatenate([t, jnp.cos(f * w), -jnp.sin(f * w)], axis=-1)
    z_rev = jnp.concatenate([z[:1], z[1:][::-1]], axis=0)
    return jnp.concatenate([z, z_rev], axis=-1)


def _hyena_filter(l, n2, f_w1, f_b1, f_w2, f_b2, f_w3, f_b3, f_w4, f_freq):
    emb, hid = f_w1.shape
    dh = f_w4.shape[1] // 2
    tr = min(l, 1024)
    z2 = _filter_feats(l, (emb - 1) // 2)
    bd = lambda w: jnp.kron(jnp.eye(2, dtype=F32), w)
    two = lambda v: jnp.concatenate([v, v], axis=-1)
    margs = [z2, bd(f_w1), two(f_b1)[None], bd(f_w2), two(f_b2)[None], bd(f_w3), two(f_b3)[None], two(f_freq)]
    full1 = lambda a: pl.BlockSpec(a.shape, lambda r: tuple(0 for _ in a.shape))
    hcat = pl.pallas_call(
        _filter_mlp_kernel,
        grid=(l // tr,),
        in_specs=[pl.BlockSpec((tr, 2 * emb), lambda r: (r, 0))] + [full1(a) for a in margs[1:]],
        out_specs=pl.BlockSpec((tr, 2 * hid), lambda r: (r, 0)),
        out_shape=jax.ShapeDtypeStruct((l, 2 * hid), F32),
        compiler_params=_params("parallel"),
        name="filter_mlp",
    )(*margs)
    zeros = jnp.zeros((hid, dh), F32)
    w4f = jnp.concatenate([f_w4[:, :dh], zeros], axis=0)
    w4b = jnp.concatenate([zeros, f_w4[:, dh:]], axis=0)
    max_decay = math.log(FILTER_TARGET) / FAST_DECAY
    min_decay = math.log(FILTER_TARGET) / SLOW_DECAY
    deltas = jnp.abs(jnp.linspace(min_decay, max_decay, dh, dtype=F32))[None, :]
    kern, l1 = pl.pallas_call(
        functools.partial(_filter_out_kernel, l, tr, n2),
        grid=(dh // LANES, l // tr),
        in_specs=[
            pl.BlockSpec((tr, 2 * hid), lambda c, r: (r, 0)),
            pl.BlockSpec((2 * hid, LANES), lambda c, r: (0, c)),
            pl.BlockSpec((2 * hid, LANES), lambda c, r: (0, c)),
            pl.BlockSpec((1, LANES), lambda c, r: (0, c)),
        ],
        out_specs=[pl.BlockSpec((2, _pitched_rows(tr, n2), LANES), lambda c, r: (0, r, c)),
                   pl.BlockSpec((1, LANES), lambda c, r: (0, c))],
        out_shape=[jax.ShapeDtypeStruct((2, _pitched_rows(l, n2), dh), F32), jax.ShapeDtypeStruct((1, dh), F32)],
        compiler_params=_params("parallel", "arbitrary"),
        name="filter_out",
    )(hcat, w4f, w4b, deltas)
    return kern.reshape(2 * _pitched_rows(l, n2), dh), l1


def _grid_sincos(rows, dim):
    r, col = jnp.meshgrid(jnp.arange(rows, dtype=F32), jnp.arange(GRID_W, dtype=F32), indexing='ij')
    quarter = dim // 4
    omega = 1.0 / (10000.0 ** (jnp.arange(quarter, dtype=F32) / quarter))

    def emb(pos):
        ang = pos.reshape(-1, 1) * omega[None, :]
        return jnp.concatenate([jnp.sin(ang), jnp.cos(ang)], axis=-1)

    return jnp.concatenate([emb(r), emb(col)], axis=-1)


def kernel(x, c, ctx, c_ctx, w_mod, b_mod, norm_g, w_in, hy_conv_w, hy_conv_b, f_w1, f_b1, f_w2, f_b2,
           f_w3, f_b3, f_w4, f_freq, hy_bias, rg_conv_w, rg_conv_b, rg_wa, rg_ba, rg_wx, rg_bx, rg_lam,
           br_norm_h, br_norm_r, w_out, final_g):
    B, N, D = x.shape
    L_ctx = ctx.shape[1]
    depth = w_mod.shape[0]
    dh = hy_bias.shape[1]
    dr = rg_conv_b.shape[1]
    assert B == NB and B % BB == 0 and dh == dr

    pos = _grid_sincos(N // GRID_W, D)
    cond = jnp.concatenate([c, jnp.broadcast_to(c_ctx[None], (8, D))], axis=0)
    mod = _modulation(cond, w_mod, b_mod)
    w_in_bf = w_in.astype(BF16)
    w_out_bf = w_out.astype(BF16)
    fc_lat = _fft_consts(N)
    fc_ctx = _fft_consts(L_ctx)

    n2_l, n2_c = fc_lat["n2"], fc_ctx["n2"]
    state = lambda h, rows: h[:, rows, :].transpose(1, 0, 2).reshape(NB, dr)

    xl, xc = x, ctx
    for l in range(depth):
        last = l == depth - 1
        ml = mod[l, :B].reshape(B, 1, 3 * D)
        mc = jnp.broadcast_to(mod[l, B:B + 1].reshape(1, 1, 3 * D), (B, 1, 3 * D))
        sh_l, sc_l, g_l = ml[..., :D], ml[..., D:2 * D], ml[..., 2 * D:]
        sh_c, sc_c, g_c = mc[..., :D], mc[..., D:2 * D], mc[..., 2 * D:]
        proj = (norm_g[l], w_in_bf[l], hy_conv_w[l], hy_conv_b[l], dh, dr)
        if l == 0:
            u_l, x0_l, zh_l, zr_l, xr_l, xl = _inproj(xl, pos, sh_l, sc_l, *proj, n2_l)
        else:
            u_l, x0_l, zh_l, zr_l, xr_l = _inproj(xl, None, sh_l, sc_l, *proj, n2_l)
        u_c, x0_c, zh_c, zr_c, xr_c = _inproj(xc, None, sh_c, sc_c, *proj, n2_c)
        rnn_w = _rglru_weights(rg_wa[l], rg_ba[l], rg_wx[l], rg_bx[l])
        zero = jnp.zeros((NB, dr), F32)
        hf_c, hb_c = _rglru(xr_c, rg_conv_w[l], rg_conv_b[l], *rnn_w, rg_lam[l], zero, zero)
        hf_l, hb_l = _rglru(xr_l, rg_conv_w[l], rg_conv_b[l], *rnn_w, rg_lam[l],
                            state(hf_c, slice(-NB, None)), state(hb_c, slice(0, NB)))
        filt = (f_w1[l], f_b1[l], f_w2[l], f_b2[l], f_w3[l], f_b3[l], f_w4[l], f_freq[l])
        shared = (br_norm_h[l], br_norm_r[l], w_out_bf[l])
        cv_l = _fftconv(u_l, _filter_fft(*_hyena_filter(N, n2_l, *filt), hy_bias[l], fc_lat), fc_lat)
        if not last:
            cv_c = _fftconv(u_c, _filter_fft(*_hyena_filter(L_ctx, n2_c, *filt), hy_bias[l], fc_ctx), fc_ctx)
            xc = _outproj(cv_c, x0_c, n2_c, hf_c, hb_c, zh_c, zr_c, *shared, xc, g_c, None)
        xl = _outproj(cv_l, x0_l, n2_l, hf_l, hb_l, zh_l, zr_l, *shared, xl, g_l,
                      final_g if last else None)
    return xl
```

```python
import functools
import math

import jax
import jax.numpy as jnp
import numpy as np
from jax import lax
from jax.experimental import pallas as pl
from jax.experimental.pallas import tpu as pltpu

GRID_W = 64
FILTER_TARGET = 1e-2
FAST_DECAY = 0.3
SLOW_DECAY = 1.5
RG_C = 8.0
EPS = 1e-6

LANES = 128
F32 = jnp.float32
BF16 = jnp.bfloat16
HIGHEST = lax.Precision.HIGHEST

VMEM_LIMIT_BYTES = 56 * 1024 * 1024


def _params(*sem):
    return pltpu.CompilerParams(dimension_semantics=sem, vmem_limit_bytes=VMEM_LIMIT_BYTES)


def _silu(x):
    return x * jax.nn.sigmoid(x)


def _rms(x, g):
    return x * lax.rsqrt(jnp.mean(x * x, axis=-1, keepdims=True) + EPS) * g


def _mod_kernel(s_ref, w_ref, b_ref, o_ref):
    s = _silu(s_ref[...])
    o_ref[0] = jnp.dot(s, w_ref[0], precision=HIGHEST, preferred_element_type=F32) + b_ref[0]


def _modulation(cond, w_mod, b_mod):
    depth, d, d3 = w_mod.shape
    rows = cond.shape[0]
    tn = d // 2
    return pl.pallas_call(
        _mod_kernel,
        grid=(depth, d3 // tn),
        in_specs=[
            pl.BlockSpec((rows, d), lambda l, j: (0, 0)),
            pl.BlockSpec((1, d, tn), lambda l, j: (l, 0, j)),
            pl.BlockSpec((1, 1, tn), lambda l, j: (l, 0, j)),
        ],
        out_specs=pl.BlockSpec((1, rows, tn), lambda l, j: (l, 0, j)),
        out_shape=jax.ShapeDtypeStruct((depth, rows, d3), F32),
        compiler_params=_params("parallel", "parallel"),
        name="modulation",
    )(cond, w_mod, b_mod.reshape(depth, 1, d3))


NB = 8
HALO = 8
BB = 4
GROUP = 2


def _inproj_kernel(add_pos, tm, dh, dr, n2, *refs):
    refs = list(refs)
    x_ref, xp_ref, xn_ref = refs[:3]
    refs = refs[3:]
    if add_pos:
        pos_ref, posp_ref, posn_ref = refs[:3]
        refs = refs[3:]
    (sh_ref, sc_ref, g_ref, wh_ref, wr_ref, cw_ref, cb_ref, u_ref, x0_ref, zh_ref, zr_ref, xr_ref) = refs[:12]
    j = pl.program_id(0)
    i = pl.program_id(1)
    keep_prev = (j > 0).astype(F32)
    keep_next = (j < pl.num_programs(0) - 1).astype(F32)
    seg = tm + 2 * HALO

    for b0 in range(0, BB, GROUP):
        xs = []
        for b in range(b0, b0 + GROUP):
            xb = jnp.concatenate([xp_ref[b], x_ref[b], xn_ref[b]], axis=0)
            if add_pos:
                xb = xb + jnp.concatenate([posp_ref[...], pos_ref[...], posn_ref[...]], axis=0)
                refs[12][b] = xb[HALO:HALO + tm]
            xs.append((_rms(xb, g_ref[...]) * (1.0 + sc_ref[b]) + sh_ref[b]).astype(BF16))

        ph = jnp.dot(jnp.concatenate(xs, axis=0), wh_ref[...], preferred_element_type=F32)
        for k, b in enumerate(range(b0, b0 + GROUP)):
            pb = ph[k * seg:(k + 1) * seg]
            full = jnp.concatenate([pb[:HALO] * keep_prev, pb[HALO:HALO + tm], pb[HALO + tm:] * keep_next], axis=0)
            c = cb_ref[...] + cw_ref[0:1, :] * full[HALO - 1:HALO - 1 + tm]
            c = c + cw_ref[1:2, :] * full[HALO:HALO + tm]
            c = c + cw_ref[2:3, :] * full[HALO + 1:HALO + 1 + tm]
            x0_ref[b] = c[:, :dh].astype(x0_ref.dtype)

            def store(r, n, v, b=b):
                u_ref[b, r:r + n, :] = v
            _store_pitched(store, c[:, 2 * dh:] * c[:, dh:2 * dh], n2)

        xn_mid = jnp.concatenate([x[HALO:HALO + tm] for x in xs], axis=0)
        pr = jnp.dot(xn_mid, wr_ref[...], preferred_element_type=F32)
        for k, b in enumerate(range(b0, b0 + GROUP)):
            pb = pr[k * tm:(k + 1) * tm]
            zh_ref[b] = pb[:, :dh].astype(zh_ref.dtype)
            zr_ref[b] = pb[:, dh + dr:].astype(zr_ref.dtype)
            for s_ in range(dr // LANES):
                xr_ref[s_, pl.ds(i * BB + b, tm, stride=NB), :] = pb[:, dh + s_ * LANES:dh + (s_ + 1) * LANES]


def _inproj(x, pos, sh, sc, g, w_bf16, conv_w, conv_b, dh, dr, n2):
    b, l, d = x.shape
    tm = min(l, 128)
    add_pos = pos is not None
    nt = l // HALO
    per = tm // HALO
    row = lambda w: pl.BlockSpec((BB, tm, w), lambda j, i: (i, j, 0))
    prev = lambda j: jnp.maximum(j * per - 1, 0)
    nxt = lambda j: jnp.minimum((j + 1) * per, nt - 1)
    vec = pl.BlockSpec((BB, 1, d), lambda j, i: (i, 0, 0))
    full = lambda a: pl.BlockSpec(a.shape, lambda j, i: tuple(0 for _ in a.shape), pipeline_mode=pl.Buffered(1))
    in_specs = [row(d),
                pl.BlockSpec((BB, HALO, d), lambda j, i: (i, prev(j), 0)),
                pl.BlockSpec((BB, HALO, d), lambda j, i: (i, nxt(j), 0))]
    args = [x, x, x]
    if add_pos:
        in_specs += [pl.BlockSpec((tm, d), lambda j, i: (j, 0)),
                     pl.BlockSpec((HALO, d), lambda j, i: (prev(j), 0)),
                     pl.BlockSpec((HALO, d), lambda j, i: (nxt(j), 0))]
        args += [pos, pos, pos]
    w_h = w_bf16[:, :3 * dh]
    w_r = w_bf16[:, 3 * dh:]
    consts = [g.reshape(1, d), w_h, w_r, conv_w, conv_b.reshape(1, 3 * dh)]
    in_specs += [vec, vec] + [full(a) for a in consts]
    args += [sh, sc] + consts
    out_specs = [pl.BlockSpec((BB, _pitched_rows(tm, n2), dh), lambda j, i: (i, j, 0)),
                 row(dh), row(dh), row(dr),
                 pl.BlockSpec((dr // LANES, tm * NB, LANES), lambda j, i: (0, j, 0))]
    out_shape = [jax.ShapeDtypeStruct((b, _pitched_rows(l, n2), dh), F32),
                 jax.ShapeDtypeStruct((b, l, dh), BF16), jax.ShapeDtypeStruct((b, l, dh), BF16),
                 jax.ShapeDtypeStruct((b, l, dr), BF16),
                 jax.ShapeDtypeStruct((dr // LANES, l * NB, LANES), F32)]
    if add_pos:
        out_specs.append(row(d))
        out_shape.append(jax.ShapeDtypeStruct((b, l, d), F32))
    return pl.pallas_call(
        functools.partial(_inproj_kernel, add_pos, tm, dh, dr, n2),
        grid=(l // tm, b // BB),
        in_specs=in_specs,
        out_specs=out_specs,
        out_shape=out_shape,
        compiler_params=_params("arbitrary", "arbitrary"),
        name="inproj_pos" if add_pos else "inproj",
    )(*args)


def _outproj_kernel(final, tm, n2, *refs):
    if final:
        (cv_ref, x0_ref, hf_ref, hb_ref, zh_ref, zr_ref, gnh_ref, gnr_ref, wt_ref, wb_ref,
         x_ref, gate_ref, fg_ref, o_ref) = refs
    else:
        (cv_ref, x0_ref, hf_ref, hb_ref, zh_ref, zr_ref, gnh_ref, gnr_ref, wt_ref, wb_ref,
         x_ref, gate_ref, o_ref) = refs
    i = pl.program_id(1)
    yhs, yrs = [], []
    for b in range(BB):
        cv = _load_pitched(lambda r, n, b=b: cv_ref[b, r:r + n, :], tm, n2)
        y_h = cv * x0_ref[b].astype(F32)
        yhs.append((_rms(y_h, gnh_ref[...]) * _silu(zh_ref[b].astype(F32))).astype(BF16))
        hs = _lane_cat([hf_ref[s_, pl.ds(i * BB + b, tm, stride=NB), :] + hb_ref[s_, pl.ds(i * BB + b, tm, stride=NB), :]
                        for s_ in range(hf_ref.shape[0])])
        yrs.append((_rms(hs, gnr_ref[...]) * _silu(zr_ref[b].astype(F32))).astype(BF16))
    y = jnp.dot(jnp.concatenate(yhs, axis=0), wt_ref[...], preferred_element_type=F32)
    y = y + jnp.dot(jnp.concatenate(yrs, axis=0), wb_ref[...], preferred_element_type=F32)
    for b in range(BB):
        out = x_ref[b] + gate_ref[b] * y[b * tm:(b + 1) * tm]
        if final:
            out = _rms(out, fg_ref[...])
        o_ref[b] = out


def _outproj(conv, x0, n2, hf, hb, zh, zr, gn_h, gn_r, w_out_bf16, x, gate, final_g):
    b, l, d = x.shape
    dh = conv.shape[-1]
    dr = zr.shape[-1]
    tm = min(l, 128)
    final = final_g is not None
    row = lambda w: pl.BlockSpec((BB, tm, w), lambda j, i: (i, j, 0))
    pitched = pl.BlockSpec((BB, _pitched_rows(tm, n2), dh), lambda j, i: (i, j, 0))
    slabs = pl.BlockSpec((dr // LANES, tm * NB, LANES), lambda j, i: (0, j, 0))
    full = lambda a: pl.BlockSpec(a.shape, lambda j, i: tuple(0 for _ in a.shape), pipeline_mode=pl.Buffered(1))
    consts = [gn_h.reshape(1, dh), gn_r.reshape(1, dr), w_out_bf16[:dh], w_out_bf16[dh:]]
    in_specs = [pitched, row(dh), slabs, slabs, row(dh), row(dr)] + [full(a) for a in consts] + [
        row(d), pl.BlockSpec((BB, 1, d), lambda j, i: (i, 0, 0))]
    args = [conv, x0, hf, hb, zh, zr] + consts + [x, gate]
    if final:
        fg = final_g.reshape(1, d)
        in_specs.append(full(fg))
        args.append(fg)
    return pl.pallas_call(
        functools.partial(_outproj_kernel, final, tm, n2),
        grid=(l // tm, b // BB),
        in_specs=in_specs,
        out_specs=row(d),
        out_shape=jax.ShapeDtypeStruct((b, l, d), F32),
        compiler_params=_params("parallel", "parallel"),
        name="outproj_final" if final else "outproj",
    )(*args)


RG_PREV = 2
RG_NEXT = 1
SQRT_FLOOR = 1e-30
LOG2E = math.log2(math.e)


def _softplus(x):
    return jnp.maximum(x, 0.0) + jnp.log(1.0 + jnp.exp(-jnp.abs(x)))


def _sqrt01(y):
    return y * lax.rsqrt(jnp.maximum(y, SQRT_FLOOR))


def _rglru_kernel(tc, xf_ref, xfp_ref, xfn_ref, xb_ref, xbp_ref, xbn_ref, cw_ref, cb_ref,
                  wf_ref, bf_ref, wb_ref, bb_ref, lam_ref, h0f_ref, h0b_ref,
                  hf_ref, hb_ref, af_s, bf_s, ab_s, bb_s, cf_s, cbk_s):
    j = pl.program_id(0)
    nj = pl.num_programs(0)
    rows = tc * NB
    ns = xf_ref.shape[0]
    dr = ns * LANES
    wide = lambda ref: _lane_cat([ref[s_] for s_ in range(ns)])

    @pl.when(j == 0)
    def _():
        cf_s[...] = h0f_ref[...]
        cbk_s[...] = h0b_ref[...]

    def coeffs(x_ref, xp_ref, xn_ref, first, last, w_ref, b_ref, lam, a_s, b_s):
        prev = jnp.where(first, 0.0, wide(xp_ref))
        nxt = jnp.where(last, 0.0, wide(xn_ref))
        xfull = jnp.concatenate([prev, wide(x_ref), nxt], axis=0)
        xh = cb_ref[...] + cw_ref[0:1, :] * xfull[0:rows]
        for k in range(1, RG_PREV + RG_NEXT + 1):
            xh = xh + cw_ref[k:k + 1, :] * xfull[k * NB:k * NB + rows]
        g = jnp.dot(xh.astype(BF16), w_ref[...], preferred_element_type=F32) + b_ref[...]
        t_r = jnp.tanh(g[:, :dr])
        t_i = jnp.tanh(g[:, dr:])
        c1 = (-0.5 * RG_C * LOG2E) * _softplus(-lam)
        a = jnp.exp2(c1 * t_r + c1)
        a_s[...] = a
        b_s[...] = _sqrt01(1.0 - a * a) * (t_i * xh + xh)

    coeffs(xf_ref, xfp_ref, xfn_ref, j == 0, j == nj - 1, wf_ref, bf_ref, lam_ref[0:1, :], af_s, bf_s)
    coeffs(xb_ref, xbp_ref, xbn_ref, j == nj - 1, j == 0, wb_ref, bb_ref, lam_ref[1:2, :], ab_s, bb_s)

    def step(t, carry):
        hf, hb = carry
        rf = pl.multiple_of(t * NB, NB)
        rb = pl.multiple_of((tc - 1 - t) * NB, NB)
        hf = af_s[pl.ds(rf, NB), :] * hf + bf_s[pl.ds(rf, NB), :]
        hb = ab_s[pl.ds(rb, NB), :] * hb + bb_s[pl.ds(rb, NB), :]
        for s_ in range(ns):
            hf_ref[s_, pl.ds(rf, NB), :] = _lane_part(hf, s_)
            hb_ref[s_, pl.ds(rb, NB), :] = _lane_part(hb, s_)
        return hf, hb

    hf, hb = lax.fori_loop(0, tc, step, (cf_s[...], cbk_s[...]), unroll=8)
    cf_s[...] = hf
    cbk_s[...] = hb


def _rglru(xr2, conv_w, conv_b, w_f, b_f, w_b, b_b, lam, h0_f, h0_b):
    ns, rows_total, _ = xr2.shape
    dr = ns * LANES
    l = rows_total // NB
    tc = min(l, 128)
    nj = l // tc
    rows = tc * NB
    prev_rows = RG_PREV * NB
    last_prev = rows // prev_rows
    n_next = rows_total // NB

    def blk(shape, fn):
        return pl.BlockSpec(shape, fn)

    full = lambda shape: pl.BlockSpec(shape, lambda j: tuple(0 for _ in shape))
    in_specs = [
        blk((ns, rows, LANES), lambda j: (0, j, 0)),
        blk((ns, prev_rows, LANES), lambda j: (0, jnp.maximum(j * last_prev - 1, 0), 0)),
        blk((ns, NB, LANES), lambda j: (0, jnp.minimum((j + 1) * tc, n_next - 1), 0)),
        blk((ns, rows, LANES), lambda j: (0, nj - 1 - j, 0)),
        blk((ns, prev_rows, LANES), lambda j: (0, jnp.maximum((nj - 1 - j) * last_prev - 1, 0), 0)),
        blk((ns, NB, LANES), lambda j: (0, jnp.minimum((nj - j) * tc, n_next - 1), 0)),
        full(conv_w.shape), full((1, dr)),
        full(w_f.shape), full((1, 2 * dr)), full(w_b.shape), full((1, 2 * dr)),
        full(lam.shape), full((NB, dr)), full((NB, dr)),
    ]
    out_specs = [blk((ns, rows, LANES), lambda j: (0, j, 0)), blk((ns, rows, LANES), lambda j: (0, nj - 1 - j, 0))]
    out_shape = [jax.ShapeDtypeStruct((ns, rows_total, LANES), F32)] * 2
    scratch = [pltpu.VMEM((rows, dr), F32)] * 4 + [pltpu.VMEM((NB, dr), F32)] * 2
    return pl.pallas_call(
        functools.partial(_rglru_kernel, tc),
        grid=(nj,),
        in_specs=in_specs,
        out_specs=out_specs,
        out_shape=out_shape,
        scratch_shapes=scratch,
        compiler_params=_params("arbitrary"),
        name="rglru",
    )(xr2, xr2, xr2, xr2, xr2, xr2, 0.5 * conv_w, 0.5 * conv_b.reshape(1, dr), w_f, 0.5 * b_f, w_b, 0.5 * b_b,
      lam, h0_f, h0_b)


def _block_diag(w):
    heads, blk, _ = w.shape
    eye = jnp.eye(heads, dtype=w.dtype)
    return jnp.einsum('hij,hg->higj', w, eye).reshape(heads * blk, heads * blk)


def _rglru_weights(wa, ba, wx, bx):
    out = []
    for d in range(2):
        w = jnp.concatenate([_block_diag(wa[d]), _block_diag(wx[d])], axis=1).astype(BF16)
        b = jnp.concatenate([ba[d], bx[d]])[None, :]
        out += [w, b]
    return out


TW_B = 8
ROW_PAD = 8
SLAB_PAD = 4
FFT_COLS = 16
FFT_SLABS = 16


def _fft_dims(l):
    n = 2 * l
    n1 = 16
    while (2 * n1) * (2 * n1) <= n:
        n1 *= 2
    return n1, n // n1


def _pitched_rows(rows, n2):
    return rows // n2 * (n2 + ROW_PAD)


def _store_pitched(store, val, n2):
    p = n2 + ROW_PAD
    pad = jnp.zeros((ROW_PAD, val.shape[1]), val.dtype)
    for g in range(val.shape[0] // n2):
        store(g * p, n2, val[g * n2:(g + 1) * n2])
        store(g * p + n2, ROW_PAD, pad)


def _load_pitched(load, rows, n2):
    p = n2 + ROW_PAD
    return jnp.concatenate([load(g * p, n2) for g in range(rows // n2)], axis=0)


def _slab_pitch(n2):
    return 2 * n2 + SLAB_PAD


def _fft_consts(l):
    n1, n2 = _fft_dims(l)
    n = n1 * n2
    h = n1 // 2
    f1 = np.exp(-2j * np.pi * np.outer(np.arange(n1), np.arange(n1)) / n1)
    f2 = np.exp(-2j * np.pi * np.outer(np.arange(n2), np.arange(n2)) / n2)
    stack = lambda a: np.block([[a.real, -a.imag], [a.imag, a.real]])
    bf = lambda a: jnp.asarray(a, F32).astype(BF16)

    g, r = np.divmod(np.arange(2 * n2), 16)
    perm = np.where(r < 8, 8 * g + r, n2 + 8 * g + (r - 8))
    inter = lambda a: a[perm][:, perm]

    m1_full = stack(f1)
    m1_half = np.concatenate([m1_full[:, :h], m1_full[:, n1:n1 + h]], axis=1)
    m3_full = stack(np.conj(f1))
    m3 = np.concatenate([m3_full[:h], m3_full[n1:n1 + h]], axis=0)
    na = n1 // TW_B
    ang_a = -2.0 * np.pi * np.outer(np.arange(na) * TW_B, np.arange(n2)) / n
    ang_b = -2.0 * np.pi * np.outer(np.arange(TW_B), np.arange(n2)) / n
    ta = np.concatenate([np.cos(ang_a), np.sin(ang_a)], axis=1)[:, perm]
    ta = jnp.asarray(np.broadcast_to(ta[..., None], ta.shape + (LANES,)), F32)
    tw_b = np.exp(1j * ang_b)
    fwd = [inter(stack(f2 * tw_b[b][None, :])) for b in range(TW_B)]
    inv = [inter(stack(np.conj(f2) * np.conj(tw_b[b])[:, None])) for b in range(TW_B)]
    return dict(n1=n1, n2=n2, m1=bf(m1_half), m3=bf(m3), ta=ta,
                m2b=jnp.stack([bf(m) for m in fwd]), m2ib=jnp.stack([bf(m) for m in inv]),
                m1_real=bf(m1_full[:, :n1]))


def _mm(m, x):
    return jnp.dot(m, x.astype(BF16), preferred_element_type=F32)


def _lane_cat(parts):
    return parts[0] if len(parts) == 1 else jnp.concatenate(parts, axis=1)


def _lane_part(x, i):
    return x[:, i * LANES:(i + 1) * LANES]


def _split_ri(x):
    x4 = x.reshape(x.shape[0] // 16, 2, 8, x.shape[1])
    return x4[:, 0], x4[:, 1]


def _join_ri(re, im):
    return jnp.stack([re, im], axis=1).reshape(2 * re.shape[0] * 8, re.shape[2])


def _cmul(x, t, conj=False):
    xr, xi = _split_ri(x)
    tr, ti = _split_ri(t)
    if conj:
        return _join_ri(xr * tr + xi * ti, xi * tr - xr * ti)
    return _join_ri(xr * tr - xi * ti, xr * ti + xi * tr)


def _col_rows(c):
    return (c // 8) * 16 + c % 8


def _stage1(n_in, n1, n2, m_ref, load, a_ref):
    cols = min(FFT_COLS, n2)
    sp = _slab_pitch(n2)

    def body(i, carry):
        out = _mm(m_ref[...], _lane_cat([load(i * cols + q) for q in range(cols)]))
        for q in range(cols):
            r = _col_rows(i * cols + q)
            a_ref[pl.ds(r, n1, stride=sp), :] = _lane_part(out[:n1], q)
            a_ref[pl.ds(r + 8, n1, stride=sp), :] = _lane_part(out[n1:], q)
        return carry

    lax.fori_loop(0, n2 // cols, body, 0)


def _slab_rows(n2, a, b):
    return pl.multiple_of((a * TW_B + b) * _slab_pitch(n2), SLAB_PAD)


def _fftconv_kernel(n1, n2, u_hbm, ks_ref, m1_ref, m2b_ref, m2ib_ref, m3_ref, ta_ref, y_hbm,
                    a_ref, buf, sem_in, sem_out):
    h = n1 // 2
    pitch = n2 + ROW_PAD
    sp = _slab_pitch(n2)
    na = n1 // TW_B
    slabs = min(FFT_SLABS, na)
    groups = na // slabs
    cols = min(FFT_COLS, n2)
    npairs = pl.num_programs(1)
    total = pl.num_programs(0) * npairs
    t = pl.program_id(0) * npairs + pl.program_id(1)
    slot = t % 2

    def window(ref, tt):
        lane0 = pl.multiple_of((tt // npairs) * LANES, LANES)
        return ref.at[pl.ds(2 * (tt % npairs), 2), :, pl.ds(lane0, LANES)]

    def in_copy(tt, s):
        return pltpu.make_async_copy(window(u_hbm, tt), buf.at[s], sem_in.at[s])

    def out_copy(tt, s):
        return pltpu.make_async_copy(buf.at[s], window(y_hbm, tt), sem_out.at[s])

    @pl.when(t == 0)
    def _():
        in_copy(t, slot).start()

    in_copy(t, slot).wait()
    u_ref = buf.at[slot]

    def load_u(c):
        return jnp.concatenate([u_ref[0, pl.ds(c, h, stride=pitch), :],
                                u_ref[1, pl.ds(c, h, stride=pitch), :]], axis=0)

    _stage1(h, n1, n2, m1_ref, load_u, a_ref)

    @pl.when(t >= 1)
    def _():
        out_copy(t - 1, 1 - slot).wait()

    @pl.when(t + 1 < total)
    def _():
        in_copy(t + 1, 1 - slot).start()

    def slab_stage(i, carry):
        b = i // groups
        a0 = (i % groups) * slabs
        rows = [_slab_rows(n2, a0 + q, b) for q in range(slabs)]
        xs = [_cmul(a_ref[pl.ds(rows[q], 2 * n2), :], ta_ref[a0 + q]) for q in range(slabs)]
        f = _mm(m2b_ref[b], _lane_cat(xs))
        cs = [_cmul(_lane_part(f, q), ks_ref[pl.ds(rows[q], 2 * n2), :]) for q in range(slabs)]
        d = _mm(m2ib_ref[b], _lane_cat(cs))
        for q in range(slabs):
            a_ref[pl.ds(rows[q], 2 * n2), :] = _cmul(_lane_part(d, q), ta_ref[a0 + q], conj=True)
        return carry

    lax.fori_loop(0, TW_B * groups, slab_stage, 0)

    def stage3(i, carry):
        xs = []
        for q in range(cols):
            r = _col_rows(i * cols + q)
            xs.append(jnp.concatenate([a_ref[pl.ds(r, n1, stride=sp), :],
                                       a_ref[pl.ds(r + 8, n1, stride=sp), :]], axis=0))
        out = _mm(m3_ref[...], _lane_cat(xs))
        for q in range(cols):
            c = i * cols + q
            u_ref[0, pl.ds(c, h, stride=pitch), :] = _lane_part(out[:h], q)
            u_ref[1, pl.ds(c, h, stride=pitch), :] = _lane_part(out[h:], q)
        return carry

    lax.fori_loop(0, n2 // cols, stage3, 0)

    out_copy(t, slot).start()

    @pl.when(t == total - 1)
    def _():
        out_copy(t, slot).wait()


def _const_spec(a):
    return pl.BlockSpec(a.shape, lambda *_: tuple(0 for _ in a.shape), pipeline_mode=pl.Buffered(1))


def _fftconv(u, kspec, fc):
    b, lp, dh = u.shape
    n1, n2 = fc["n1"], fc["n2"]
    nslab = n1 * _slab_pitch(n2)
    consts = [fc["m1"], fc["m2b"], fc["m2ib"], fc["m3"], fc["ta"]]
    return pl.pallas_call(
        functools.partial(_fftconv_kernel, n1, n2),
        grid=(dh // LANES, b // 2),
        in_specs=[
            pl.BlockSpec(memory_space=pl.ANY),
            pl.BlockSpec((nslab, LANES), lambda c, i: (0, c), pipeline_mode=pl.Buffered(1)),
        ] + [_const_spec(a) for a in consts],
        out_specs=pl.BlockSpec(memory_space=pl.ANY),
        out_shape=jax.ShapeDtypeStruct((b, lp, dh), F32),
        scratch_shapes=[pltpu.VMEM((nslab, LANES), F32), pltpu.VMEM((2, 2, lp, LANES), F32),
                        pltpu.SemaphoreType.DMA((2,)), pltpu.SemaphoreType.DMA((2,))],
        compiler_params=_params("arbitrary", "arbitrary"),
        name="fftconv",
    )(u, kspec, *consts)


def _filter_fft_kernel(n1, n2, k_ref, l1_ref, hyb_ref, m1_ref, m2b_ref, ta_ref, o_ref):
    scale = (1.0 / (n1 * n2)) / l1_ref[...]
    skip = hyb_ref[...] * (1.0 / (n1 * n2))
    pitch = n2 + ROW_PAD
    na = n1 // TW_B
    slabs = min(FFT_SLABS, na)
    groups = na // slabs

    _stage1(n1, n1, n2, m1_ref, lambda c: k_ref[pl.ds(c, n1, stride=pitch), :], o_ref)

    def slab_stage(i, carry):
        b = i // groups
        a0 = (i % groups) * slabs
        rows = [_slab_rows(n2, a0 + q, b) for q in range(slabs)]
        xs = [_cmul(o_ref[pl.ds(rows[q], 2 * n2), :], ta_ref[a0 + q]) for q in range(slabs)]
        f = _mm(m2b_ref[b], _lane_cat(xs))
        for q in range(slabs):
            fr, fi = _split_ri(_lane_part(f, q) * scale)
            o_ref[pl.ds(rows[q], 2 * n2), :] = _join_ri(fr + skip, fi)
            o_ref[pl.ds(rows[q] + 2 * n2, SLAB_PAD), :] = jnp.zeros((SLAB_PAD, LANES), F32)
        return carry

    lax.fori_loop(0, TW_B * groups, slab_stage, 0)


def _filter_fft(kern, l1, hy_bias, fc):
    npad, dh = kern.shape
    n1, n2 = fc["n1"], fc["n2"]
    nslab = n1 * _slab_pitch(n2)
    consts = [fc["m1_real"], fc["m2b"], fc["ta"]]
    return pl.pallas_call(
        functools.partial(_filter_fft_kernel, n1, n2),
        grid=(dh // LANES,),
        in_specs=[pl.BlockSpec((npad, LANES), lambda c: (0, c), pipeline_mode=pl.Buffered(1)),
                  pl.BlockSpec((1, LANES), lambda c: (0, c)),
                  pl.BlockSpec((1, LANES), lambda c: (0, c))]
        + [_const_spec(a) for a in consts],
        out_specs=pl.BlockSpec((nslab, LANES), lambda c: (0, c)),
        out_shape=jax.ShapeDtypeStruct((nslab, dh), F32),
        compiler_params=_params("parallel"),
        name="filter_fft",
    )(kern, l1, hy_bias.reshape(1, dh), *consts)


def _filter_mlp_kernel(z_ref, w1_ref, b1_ref, w2_ref, b2_ref, w3_ref, b3_ref, fr_ref, h_ref):
    dot = lambda a, b: jnp.dot(a, b, precision=HIGHEST, preferred_element_type=F32)
    h = jnp.sin(fr_ref[0:1, :] * (dot(z_ref[...], w1_ref[...]) + b1_ref[...]))
    h = jnp.sin(fr_ref[1:2, :] * (dot(h, w2_ref[...]) + b2_ref[...]))
    h_ref[...] = jnp.sin(fr_ref[2:3, :] * (dot(h, w3_ref[...]) + b3_ref[...]))


def _filter_out_kernel(l, tr, n2, h_ref, w4f_ref, w4b_ref, dl_ref, o_ref, l1_ref):
    rc = pl.program_id(1)
    hcat = h_ref[...]
    hf = jnp.dot(hcat, w4f_ref[...], precision=HIGHEST, preferred_element_type=F32)
    hb = jnp.dot(hcat, w4b_ref[...], precision=HIGHEST, preferred_element_type=F32)
    row = lax.broadcasted_iota(jnp.int32, (tr, LANES), 0) + rc * tr
    step = 1.0 / (l - 1)
    t_f = row.astype(F32) * step
    t_b = (l - row).astype(F32) * step
    kf = hf * jnp.exp(-t_f * dl_ref[...])
    kb = jnp.where(row == 0, 0.0, hb * jnp.exp(-t_b * dl_ref[...]))
    for half, val in ((0, kf), (1, kb)):
        def store(r, n, v, half=half):
            o_ref[half, r:r + n, :] = v
        _store_pitched(store, val, n2)
    part = jnp.sum(jnp.abs(kf), axis=0, keepdims=True) + jnp.sum(jnp.abs(kb), axis=0, keepdims=True)

    @pl.when(rc == 0)
    def _():
        l1_ref[...] = part

    @pl.when(rc > 0)
    def _():
        l1_ref[...] += part


def _filter_feats(l, bands):
    t = jnp.linspace(0.0, 1.0, l, dtype=F32)[:, None]
    w = 2.0 * math.pi * jnp.arange(l, dtype=F32)[:, None] / l
    f = jnp.linspace(1e-4, bands - 1, bands, dtype=F32)[None, :]
    z = jnp.concatenate([t, jnp.cos(f * w), -jnp.sin(f * w)], axis=-1)
    z_rev = jnp.concatenate([z[:1], z[1:][::-1]], axis=0)
    return jnp.concatenate([z, z_rev], axis=-1)


def _hyena_filter(l, n2, f_w1, f_b1, f_w2, f_b2, f_w3, f_b3, f_w4, f_freq):
    emb, hid = f_w1.shape
    dh = f_w4.shape[1] // 2
    tr = min(l, 1024)
    z2 = _filter_feats(l, (emb - 1) // 2)
    bd = lambda w: jnp.kron(jnp.eye(2, dtype=F32), w)
    two = lambda v: jnp.concatenate([v, v], axis=-1)
    margs = [z2, bd(f_w1), two(f_b1)[None], bd(f_w2), two(f_b2)[None], bd(f_w3), two(f_b3)[None], two(f_freq)]
    full1 = lambda a: pl.BlockSpec(a.shape, lambda r: tuple(0 for _ in a.shape))
    hcat = pl.pallas_call(
        _filter_mlp_kernel,
        grid=(l // tr,),
        in_specs=[pl.BlockSpec((tr, 2 * emb), lambda r: (r, 0))] + [full1(a) for a in margs[1:]],
        out_specs=pl.BlockSpec((tr, 2 * hid), lambda r: (r, 0)),
        out_shape=jax.ShapeDtypeStruct((l, 2 * hid), F32),
        compiler_params=_params("parallel"),
        name="filter_mlp",
    )(*margs)
    zeros = jnp.zeros((hid, dh), F32)
    w4f = jnp.concatenate([f_w4[:, :dh], zeros], axis=0)
    w4b = jnp.concatenate([zeros, f_w4[:, dh:]], axis=0)
    max_decay = math.log(FILTER_TARGET) / FAST_DECAY
    min_decay = math.log(FILTER_TARGET) / SLOW_DECAY
    deltas = jnp.abs(jnp.linspace(min_decay, max_decay, dh, dtype=F32))[None, :]
    kern, l1 = pl.pallas_call(
        functools.partial(_filter_out_kernel, l, tr, n2),
        grid=(dh // LANES, l // tr),
        in_specs=[
            pl.BlockSpec((tr, 2 * hid), lambda c, r: (r, 0)),
            pl.BlockSpec((2 * hid, LANES), lambda c, r: (0, c)),
            pl.BlockSpec((2 * hid, LANES), lambda c, r: (0, c)),
            pl.BlockSpec((1, LANES), lambda c, r: (0, c)),
        ],
        out_specs=[pl.BlockSpec((2, _pitched_rows(tr, n2), LANES), lambda c, r: (0, r, c)),
                   pl.BlockSpec((1, LANES), lambda c, r: (0, c))],
        out_shape=[jax.ShapeDtypeStruct((2, _pitched_rows(l, n2), dh), F32), jax.ShapeDtypeStruct((1, dh), F32)],
        compiler_params=_params("parallel", "arbitrary"),
        name="filter_out",
    )(hcat, w4f, w4b, deltas)
    return kern.reshape(2 * _pitched_rows(l, n2), dh), l1


def _grid_sincos(rows, dim):
    r, col = jnp.meshgrid(jnp.arange(rows, dtype=F32), jnp.arange(GRID_W, dtype=F32), indexing='ij')
    quarter = dim // 4
    omega = 1.0 / (10000.0 ** (jnp.arange(quarter, dtype=F32) / quarter))

    def emb(pos):
        ang = pos.reshape(-1, 1) * omega[None, :]
        return jnp.concatenate([jnp.sin(ang), jnp.cos(ang)], axis=-1)

    return jnp.concatenate([emb(r), emb(col)], axis=-1)


def kernel(x, c, ctx, c_ctx, w_mod, b_mod, norm_g, w_in, hy_conv_w, hy_conv_b, f_w1, f_b1, f_w2, f_b2,
           f_w3, f_b3, f_w4, f_freq, hy_bias, rg_conv_w, rg_conv_b, rg_wa, rg_ba, rg_wx, rg_bx, rg_lam,
           br_norm_h, br_norm_r, w_out, final_g):
    B, N, D = x.shape
    L_ctx = ctx.shape[1]
    depth = w_mod.shape[0]
    dh = hy_bias.shape[1]
    dr = rg_conv_b.shape[1]
    assert B == NB and B % BB == 0 and dh == dr

    pos = _grid_sincos(N // GRID_W, D)
    cond = jnp.concatenate([c, jnp.broadcast_to(c_ctx[None], (8, D))], axis=0)
    mod = _modulation(cond, w_mod, b_mod)
    w_in_bf = w_in.astype(BF16)
    w_out_bf = w_out.astype(BF16)
    fc_lat = _fft_consts(N)
    fc_ctx = _fft_consts(L_ctx)

    n2_l, n2_c = fc_lat["n2"], fc_ctx["n2"]
    state = lambda h, rows: h[:, rows, :].transpose(1, 0, 2).reshape(NB, dr)

    xl, xc = x, ctx
    for l in range(depth):
        last = l == depth - 1
        ml = mod[l, :B].reshape(B, 1, 3 * D)
        mc = jnp.broadcast_to(mod[l, B:B + 1].reshape(1, 1, 3 * D), (B, 1, 3 * D))
        sh_l, sc_l, g_l = ml[..., :D], ml[..., D:2 * D], ml[..., 2 * D:]
        sh_c, sc_c, g_c = mc[..., :D], mc[..., D:2 * D], mc[..., 2 * D:]
        proj = (norm_g[l], w_in_bf[l], hy_conv_w[l], hy_conv_b[l], dh, dr)
        if l == 0:
            u_l, x0_l, zh_l, zr_l, xr_l, xl = _inproj(xl, pos, sh_l, sc_l, *proj, n2_l)
        else:
            u_l, x0_l, zh_l, zr_l, xr_l = _inproj(xl, None, sh_l, sc_l, *proj, n2_l)
        u_c, x0_c, zh_c, zr_c, xr_c = _inproj(xc, None, sh_c, sc_c, *proj, n2_c)
        rnn_w = _rglru_weights(rg_wa[l], rg_ba[l], rg_wx[l], rg_bx[l])
        zero = jnp.zeros((NB, dr), F32)
        hf_c, hb_c = _rglru(xr_c, rg_conv_w[l], rg_conv_b[l], *rnn_w, rg_lam[l], zero, zero)
        hf_l, hb_l = _rglru(xr_l, rg_conv_w[l], rg_conv_b[l], *rnn_w, rg_lam[l],
                            state(hf_c, slice(-NB, None)), state(hb_c, slice(0, NB)))
        filt = (f_w1[l], f_b1[l], f_w2[l], f_b2[l], f_w3[l], f_b3[l], f_w4[l], f_freq[l])
        shared = (br_norm_h[l], br_norm_r[l], w_out_bf[l])
        cv_l = _fftconv(u_l, _filter_fft(*_hyena_filter(N, n2_l, *filt), hy_bias[l], fc_lat), fc_lat)
        if not last:
            cv_c = _fftconv(u_c, _filter_fft(*_hyena_filter(L_ctx, n2_c, *filt), hy_bias[l], fc_ctx), fc_ctx)
            xc = _outproj(cv_c, x0_c, n2_c, hf_c, hb_c, zh_c, zr_c, *shared, xc, g_c, None)
        xl = _outproj(cv_l, x0_l, n2_l, hf_l, hb_l, zh_l, zr_l, *shared, xl, g_l,
                      final_g if last else None)
    return xl
```

```python
import functools
import math

import jax
import jax.numpy as jnp
import numpy as np
from jax import lax
from jax.experimental import pallas as pl
from jax.experimental.pallas import tpu as pltpu

GRID_W = 64
FILTER_TARGET = 1e-2
FAST_DECAY = 0.3
SLOW_DECAY = 1.5
RG_C = 8.0
EPS = 1e-6

LANES = 128
F32 = jnp.float32
BF16 = jnp.bfloat16
HIGHEST = lax.Precision.HIGHEST

VMEM_LIMIT_BYTES = 56 * 1024 * 1024


def _params(*sem):
    return pltpu.CompilerParams(dimension_semantics=sem, vmem_limit_bytes=VMEM_LIMIT_BYTES)


def _silu(x):
    return x * jax.nn.sigmoid(x)


def _rms(x, g):
    return x * lax.rsqrt(jnp.mean(x * x, axis=-1, keepdims=True) + EPS) * g


def _mod_kernel(s_ref, w_ref, b_ref, o_ref):
    s = _silu(s_ref[...])
    o_ref[0] = jnp.dot(s, w_ref[0], precision=HIGHEST, preferred_element_type=F32) + b_ref[0]


def _modulation(cond, w_mod, b_mod):
    depth, d, d3 = w_mod.shape
    rows = cond.shape[0]
    tn = d // 2
    return pl.pallas_call(
        _mod_kernel,
        grid=(depth, d3 // tn),
        in_specs=[
            pl.BlockSpec((rows, d), lambda l, j: (0, 0)),
            pl.BlockSpec((1, d, tn), lambda l, j: (l, 0, j)),
            pl.BlockSpec((1, 1, tn), lambda l, j: (l, 0, j)),
        ],
        out_specs=pl.BlockSpec((1, rows, tn), lambda l, j: (l, 0, j)),
        out_shape=jax.ShapeDtypeStruct((depth, rows, d3), F32),
        compiler_params=_params("parallel", "parallel"),
        name="modulation",
    )(cond, w_mod, b_mod.reshape(depth, 1, d3))


NB = 8
HALO = 8
BB = 4
GROUP = 2


def _inproj_kernel(add_pos, tm, dh, dr, n2, *refs):
    refs = list(refs)
    x_ref, xp_ref, xn_ref = refs[:3]
    refs = refs[3:]
    if add_pos:
        pos_ref, posp_ref, posn_ref = refs[:3]
        refs = refs[3:]
    (sh_ref, sc_ref, g_ref, wh_ref, wr_ref, cw_ref, cb_ref, u_ref, x0_ref, zh_ref, zr_ref, xr_ref) = refs[:12]
    j = pl.program_id(0)
    i = pl.program_id(1)
    keep_prev = (j > 0).astype(F32)
    keep_next = (j < pl.num_programs(0) - 1).astype(F32)
    seg = tm + 2 * HALO

    for b0 in range(0, BB, GROUP):
        xs = []
        for b in range(b0, b0 + GROUP):
            xb = jnp.concatenate([xp_ref[b], x_ref[b], xn_ref[b]], axis=0)
            if add_pos:
                xb = xb + jnp.concatenate([posp_ref[...], pos_ref[...], posn_ref[...]], axis=0)
                refs[12][b] = xb[HALO:HALO + tm]
            xs.append((_rms(xb, g_ref[...]) * (1.0 + sc_ref[b]) + sh_ref[b]).astype(BF16))

        ph = jnp.dot(jnp.concatenate(xs, axis=0), wh_ref[...], preferred_element_type=F32)
        for k, b in enumerate(range(b0, b0 + GROUP)):
            pb = ph[k * seg:(k + 1) * seg]
            full = jnp.concatenate([pb[:HALO] * keep_prev, pb[HALO:HALO + tm], pb[HALO + tm:] * keep_next], axis=0)
            c = cb_ref[...] + cw_ref[0:1, :] * full[HALO - 1:HALO - 1 + tm]
            c = c + cw_ref[1:2, :] * full[HALO:HALO + tm]
            c = c + cw_ref[2:3, :] * full[HALO + 1:HALO + 1 + tm]
            x0_ref[b] = c[:, :dh].astype(x0_ref.dtype)

            def store(r, n, v, b=b):
                for s_ in range(dh // LANES):
                    u_ref[s_, b, r:r + n, :] = _lane_part(v, s_)
            _store_pitched(store, c[:, 2 * dh:] * c[:, dh:2 * dh], n2)

        xn_mid = jnp.concatenate([x[HALO:HALO + tm] for x in xs], axis=0)
        pr = jnp.dot(xn_mid, wr_ref[...], preferred_element_type=F32)
        for k, b in enumerate(range(b0, b0 + GROUP)):
            pb = pr[k * tm:(k + 1) * tm]
            zh_ref[b] = pb[:, :dh].astype(zh_ref.dtype)
            zr_ref[b] = pb[:, dh + dr:].astype(zr_ref.dtype)
            for s_ in range(dr // LANES):
                xr_ref[s_, pl.ds(i * BB + b, tm, stride=NB), :] = pb[:, dh + s_ * LANES:dh + (s_ + 1) * LANES]


def _inproj(x, pos, sh, sc, g, w_bf16, conv_w, conv_b, dh, dr, n2):
    b, l, d = x.shape
    tm = min(l, 128)
    add_pos = pos is not None
    nt = l // HALO
    per = tm // HALO
    row = lambda w: pl.BlockSpec((BB, tm, w), lambda j, i: (i, j, 0))
    prev = lambda j: jnp.maximum(j * per - 1, 0)
    nxt = lambda j: jnp.minimum((j + 1) * per, nt - 1)
    vec = pl.BlockSpec((BB, 1, d), lambda j, i: (i, 0, 0))
    full = lambda a: pl.BlockSpec(a.shape, lambda j, i: tuple(0 for _ in a.shape), pipeline_mode=pl.Buffered(1))
    in_specs = [row(d),
                pl.BlockSpec((BB, HALO, d), lambda j, i: (i, prev(j), 0)),
                pl.BlockSpec((BB, HALO, d), lambda j, i: (i, nxt(j), 0))]
    args = [x, x, x]
    if add_pos:
        in_specs += [pl.BlockSpec((tm, d), lambda j, i: (j, 0)),
                     pl.BlockSpec((HALO, d), lambda j, i: (prev(j), 0)),
                     pl.BlockSpec((HALO, d), lambda j, i: (nxt(j), 0))]
        args += [pos, pos, pos]
    w_h = w_bf16[:, :3 * dh]
    w_r = w_bf16[:, 3 * dh:]
    consts = [g.reshape(1, d), w_h, w_r, conv_w, conv_b.reshape(1, 3 * dh)]
    in_specs += [vec, vec] + [full(a) for a in consts]
    args += [sh, sc] + consts
    out_specs = [pl.BlockSpec((dh // LANES, BB, _pitched_rows(tm, n2), LANES), lambda j, i: (0, i, j, 0)),
                 row(dh), row(dh), row(dr),
                 pl.BlockSpec((dr // LANES, tm * NB, LANES), lambda j, i: (0, j, 0))]
    out_shape = [jax.ShapeDtypeStruct((dh // LANES, b, _pitched_rows(l, n2), LANES), F32),
                 jax.ShapeDtypeStruct((b, l, dh), BF16), jax.ShapeDtypeStruct((b, l, dh), BF16),
                 jax.ShapeDtypeStruct((b, l, dr), BF16),
                 jax.ShapeDtypeStruct((dr // LANES, l * NB, LANES), F32)]
    if add_pos:
        out_specs.append(row(d))
        out_shape.append(jax.ShapeDtypeStruct((b, l, d), F32))
    return pl.pallas_call(
        functools.partial(_inproj_kernel, add_pos, tm, dh, dr, n2),
        grid=(l // tm, b // BB),
        in_specs=in_specs,
        out_specs=out_specs,
        out_shape=out_shape,
        compiler_params=_params("arbitrary", "arbitrary"),
        name="inproj_pos" if add_pos else "inproj",
    )(*args)


def _outproj_kernel(final, tm, n2, *refs):
    if final:
        (cv_ref, x0_ref, hf_ref, hb_ref, zh_ref, zr_ref, gnh_ref, gnr_ref, wt_ref, wb_ref,
         x_ref, gate_ref, fg_ref, o_ref) = refs
    else:
        (cv_ref, x0_ref, hf_ref, hb_ref, zh_ref, zr_ref, gnh_ref, gnr_ref, wt_ref, wb_ref,
         x_ref, gate_ref, o_ref) = refs
    i = pl.program_id(1)
    yhs, yrs = [], []
    for b in range(BB):
        cv = _lane_cat([_load_pitched(lambda r, n, b=b, s_=s_: cv_ref[s_, b, r:r + n, :], tm, n2)
                        for s_ in range(cv_ref.shape[0])])
        y_h = cv * x0_ref[b].astype(F32)
        yhs.append((_rms(y_h, gnh_ref[...]) * _silu(zh_ref[b].astype(F32))).astype(BF16))
        hs = _lane_cat([hf_ref[s_, pl.ds(i * BB + b, tm, stride=NB), :] + hb_ref[s_, pl.ds(i * BB + b, tm, stride=NB), :]
                        for s_ in range(hf_ref.shape[0])])
        yrs.append((_rms(hs, gnr_ref[...]) * _silu(zr_ref[b].astype(F32))).astype(BF16))
    y = jnp.dot(jnp.concatenate(yhs, axis=0), wt_ref[...], preferred_element_type=F32)
    y = y + jnp.dot(jnp.concatenate(yrs, axis=0), wb_ref[...], preferred_element_type=F32)
    for b in range(BB):
        out = x_ref[b] + gate_ref[b] * y[b * tm:(b + 1) * tm]
        if final:
            out = _rms(out, fg_ref[...])
        o_ref[b] = out


def _outproj(conv, x0, n2, hf, hb, zh, zr, gn_h, gn_r, w_out_bf16, x, gate, final_g):
    b, l, d = x.shape
    dh = x0.shape[-1]
    dr = zr.shape[-1]
    tm = min(l, 128)
    final = final_g is not None
    row = lambda w: pl.BlockSpec((BB, tm, w), lambda j, i: (i, j, 0))
    pitched = pl.BlockSpec((dh // LANES, BB, _pitched_rows(tm, n2), LANES), lambda j, i: (0, i, j, 0))
    slabs = pl.BlockSpec((dr // LANES, tm * NB, LANES), lambda j, i: (0, j, 0))
    full = lambda a: pl.BlockSpec(a.shape, lambda j, i: tuple(0 for _ in a.shape), pipeline_mode=pl.Buffered(1))
    consts = [gn_h.reshape(1, dh), gn_r.reshape(1, dr), w_out_bf16[:dh], w_out_bf16[dh:]]
    in_specs = [pitched, row(dh), slabs, slabs, row(dh), row(dr)] + [full(a) for a in consts] + [
        row(d), pl.BlockSpec((BB, 1, d), lambda j, i: (i, 0, 0))]
    args = [conv, x0, hf, hb, zh, zr] + consts + [x, gate]
    if final:
        fg = final_g.reshape(1, d)
        in_specs.append(full(fg))
        args.append(fg)
    return pl.pallas_call(
        functools.partial(_outproj_kernel, final, tm, n2),
        grid=(l // tm, b // BB),
        in_specs=in_specs,
        out_specs=row(d),
        out_shape=jax.ShapeDtypeStruct((b, l, d), F32),
        compiler_params=_params("parallel", "parallel"),
        name="outproj_final" if final else "outproj",
    )(*args)


RG_PREV = 2
RG_NEXT = 1
SQRT_FLOOR = 1e-30
LOG2E = math.log2(math.e)


def _softplus(x):
    return jnp.maximum(x, 0.0) + jnp.log(1.0 + jnp.exp(-jnp.abs(x)))


def _sqrt01(y):
    return y * lax.rsqrt(jnp.maximum(y, SQRT_FLOOR))


def _rglru_kernel(tc, xf_ref, xfp_ref, xfn_ref, xb_ref, xbp_ref, xbn_ref, cw_ref, cb_ref,
                  wf_ref, bf_ref, wb_ref, bb_ref, lam_ref, h0f_ref, h0b_ref,
                  hf_ref, hb_ref, af_s, bf_s, ab_s, bb_s, cf_s, cbk_s):
    j = pl.program_id(0)
    nj = pl.num_programs(0)
    rows = tc * NB
    ns = xf_ref.shape[0]
    dr = ns * LANES
    wide = lambda ref: _lane_cat([ref[s_] for s_ in range(ns)])

    @pl.when(j == 0)
    def _():
        cf_s[...] = h0f_ref[...]
        cbk_s[...] = h0b_ref[...]

    def coeffs(x_ref, xp_ref, xn_ref, first, last, w_ref, b_ref, lam, a_s, b_s):
        prev = jnp.where(first, 0.0, wide(xp_ref))
        nxt = jnp.where(last, 0.0, wide(xn_ref))
        xfull = jnp.concatenate([prev, wide(x_ref), nxt], axis=0)
        xh = cb_ref[...] + cw_ref[0:1, :] * xfull[0:rows]
        for k in range(1, RG_PREV + RG_NEXT + 1):
            xh = xh + cw_ref[k:k + 1, :] * xfull[k * NB:k * NB + rows]
        g = jnp.dot(xh.astype(BF16), w_ref[...], preferred_element_type=F32) + b_ref[...]
        t_r = jnp.tanh(g[:, :dr])
        t_i = jnp.tanh(g[:, dr:])
        c1 = (-0.5 * RG_C * LOG2E) * _softplus(-lam)
        a = jnp.exp2(c1 * t_r + c1)
        a_s[...] = a
        b_s[...] = _sqrt01(1.0 - a * a) * (t_i * xh + xh)

    coeffs(xf_ref, xfp_ref, xfn_ref, j == 0, j == nj - 1, wf_ref, bf_ref, lam_ref[0:1, :], af_s, bf_s)
    coeffs(xb_ref, xbp_ref, xbn_ref, j == nj - 1, j == 0, wb_ref, bb_ref, lam_ref[1:2, :], ab_s, bb_s)

    def step(t, carry):
        hf, hb = carry
        rf = pl.multiple_of(t * NB, NB)
        rb = pl.multiple_of((tc - 1 - t) * NB, NB)
        hf = af_s[pl.ds(rf, NB), :] * hf + bf_s[pl.ds(rf, NB), :]
        hb = ab_s[pl.ds(rb, NB), :] * hb + bb_s[pl.ds(rb, NB), :]
        for s_ in range(ns):
            hf_ref[s_, pl.ds(rf, NB), :] = _lane_part(hf, s_)
            hb_ref[s_, pl.ds(rb, NB), :] = _lane_part(hb, s_)
        return hf, hb

    hf, hb = lax.fori_loop(0, tc, step, (cf_s[...], cbk_s[...]), unroll=8)
    cf_s[...] = hf
    cbk_s[...] = hb


def _rglru(xr2, conv_w, conv_b, w_f, b_f, w_b, b_b, lam, h0_f, h0_b):
    ns, rows_total, _ = xr2.shape
    dr = ns * LANES
    l = rows_total // NB
    tc = min(l, 128)
    nj = l // tc
    rows = tc * NB
    prev_rows = RG_PREV * NB
    last_prev = rows // prev_rows
    n_next = rows_total // NB

    def blk(shape, fn):
        return pl.BlockSpec(shape, fn)

    full = lambda shape: pl.BlockSpec(shape, lambda j: tuple(0 for _ in shape))
    in_specs = [
        blk((ns, rows, LANES), lambda j: (0, j, 0)),
        blk((ns, prev_rows, LANES), lambda j: (0, jnp.maximum(j * last_prev - 1, 0), 0)),
        blk((ns, NB, LANES), lambda j: (0, jnp.minimum((j + 1) * tc, n_next - 1), 0)),
        blk((ns, rows, LANES), lambda j: (0, nj - 1 - j, 0)),
        blk((ns, prev_rows, LANES), lambda j: (0, jnp.maximum((nj - 1 - j) * last_prev - 1, 0), 0)),
        blk((ns, NB, LANES), lambda j: (0, jnp.minimum((nj - j) * tc, n_next - 1), 0)),
        full(conv_w.shape), full((1, dr)),
        full(w_f.shape), full((1, 2 * dr)), full(w_b.shape), full((1, 2 * dr)),
        full(lam.shape), full((NB, dr)), full((NB, dr)),
    ]
    out_specs = [blk((ns, rows, LANES), lambda j: (0, j, 0)), blk((ns, rows, LANES), lambda j: (0, nj - 1 - j, 0))]
    out_shape = [jax.ShapeDtypeStruct((ns, rows_total, LANES), F32)] * 2
    scratch = [pltpu.VMEM((rows, dr), F32)] * 4 + [pltpu.VMEM((NB, dr), F32)] * 2
    return pl.pallas_call(
        functools.partial(_rglru_kernel, tc),
        grid=(nj,),
        in_specs=in_specs,
        out_specs=out_specs,
        out_shape=out_shape,
        scratch_shapes=scratch,
        compiler_params=_params("arbitrary"),
        name="rglru",
    )(xr2, xr2, xr2, xr2, xr2, xr2, 0.5 * conv_w, 0.5 * conv_b.reshape(1, dr), w_f, 0.5 * b_f, w_b, 0.5 * b_b,
      lam, h0_f, h0_b)


def _block_diag(w):
    heads, blk, _ = w.shape
    eye = jnp.eye(heads, dtype=w.dtype)
    return jnp.einsum('hij,hg->higj', w, eye).reshape(heads * blk, heads * blk)


def _rglru_weights(wa, ba, wx, bx):
    out = []
    for d in range(2):
        w = jnp.concatenate([_block_diag(wa[d]), _block_diag(wx[d])], axis=1).astype(BF16)
        b = jnp.concatenate([ba[d], bx[d]])[None, :]
        out += [w, b]
    return out


TW_B = 8
ROW_PAD = 8
SLAB_PAD = 4
FFT_COLS = 16
FFT_SLABS = 16


def _fft_dims(l):
    n = 2 * l
    n1 = 16
    while (2 * n1) * (2 * n1) <= n:
        n1 *= 2
    return n1, n // n1


def _pitched_rows(rows, n2):
    return rows // n2 * (n2 + ROW_PAD)


def _store_pitched(store, val, n2):
    p = n2 + ROW_PAD
    pad = jnp.zeros((ROW_PAD, val.shape[1]), val.dtype)
    for g in range(val.shape[0] // n2):
        store(g * p, n2, val[g * n2:(g + 1) * n2])
        store(g * p + n2, ROW_PAD, pad)


def _load_pitched(load, rows, n2):
    p = n2 + ROW_PAD
    return jnp.concatenate([load(g * p, n2) for g in range(rows // n2)], axis=0)


def _slab_pitch(n2):
    return 2 * n2 + SLAB_PAD


def _fft_consts(l):
    n1, n2 = _fft_dims(l)
    n = n1 * n2
    h = n1 // 2
    f1 = np.exp(-2j * np.pi * np.outer(np.arange(n1), np.arange(n1)) / n1)
    f2 = np.exp(-2j * np.pi * np.outer(np.arange(n2), np.arange(n2)) / n2)
    stack = lambda a: np.block([[a.real, -a.imag], [a.imag, a.real]])
    bf = lambda a: jnp.asarray(a, F32).astype(BF16)

    g, r = np.divmod(np.arange(2 * n2), 16)
    perm = np.where(r < 8, 8 * g + r, n2 + 8 * g + (r - 8))
    inter = lambda a: a[perm][:, perm]

    m1_full = stack(f1)
    m1_half = np.concatenate([m1_full[:, :h], m1_full[:, n1:n1 + h]], axis=1)
    m3_full = stack(np.conj(f1))
    m3 = np.concatenate([m3_full[:h], m3_full[n1:n1 + h]], axis=0)
    na = n1 // TW_B
    p = np.arange(n1)
    k1_of_p = TW_B * (p % na) + p // na
    order = np.concatenate([k1_of_p, n1 + k1_of_p])
    m1_half, m1_real, m3 = m1_half[order], m1_full[:, :n1][order], m3[:, order]
    ang_a = -2.0 * np.pi * np.outer(np.arange(na) * TW_B, np.arange(n2)) / n
    ang_b = -2.0 * np.pi * np.outer(np.arange(TW_B), np.arange(n2)) / n
    ta = np.concatenate([np.cos(ang_a), np.sin(ang_a)], axis=1)[:, perm]
    ta = jnp.asarray(np.broadcast_to(ta[..., None], ta.shape + (LANES,)), F32)
    tw_b = np.exp(1j * ang_b)
    fwd = [inter(stack(f2 * tw_b[b][None, :])) for b in range(TW_B)]
    inv = [inter(stack(np.conj(f2) * np.conj(tw_b[b])[:, None])) for b in range(TW_B)]
    return dict(n1=n1, n2=n2, m1=bf(m1_half), m3=bf(m3), ta=ta,
                m2b=jnp.stack([bf(m) for m in fwd]), m2ib=jnp.stack([bf(m) for m in inv]),
                m1_real=bf(m1_real))


def _mm(m, x):
    return jnp.dot(m, x.astype(BF16), preferred_element_type=F32)


def _lane_cat(parts):
    return parts[0] if len(parts) == 1 else jnp.concatenate(parts, axis=1)


def _lane_part(x, i):
    return x[:, i * LANES:(i + 1) * LANES]


def _split_ri(x):
    x4 = x.reshape(x.shape[0] // 16, 2, 8, x.shape[1])
    return x4[:, 0], x4[:, 1]


def _join_ri(re, im):
    return jnp.stack([re, im], axis=1).reshape(2 * re.shape[0] * 8, re.shape[2])


def _cmul(x, t, conj=False):
    xr, xi = _split_ri(x)
    tr, ti = _split_ri(t)
    if conj:
        return _join_ri(xr * tr + xi * ti, xi * tr - xr * ti)
    return _join_ri(xr * tr - xi * ti, xr * ti + xi * tr)


def _col_rows(c):
    return (c // 8) * 16 + c % 8


def _stage1(n_in, n1, n2, m_ref, load, a_ref):
    cols = min(FFT_COLS, n2)
    sp = _slab_pitch(n2)

    def body(i, carry):
        out = _mm(m_ref[...], _lane_cat([load(i * cols + q) for q in range(cols)]))
        for q in range(cols):
            r = _col_rows(i * cols + q)
            a_ref[pl.ds(r, n1, stride=sp), :] = _lane_part(out[:n1], q)
            a_ref[pl.ds(r + 8, n1, stride=sp), :] = _lane_part(out[n1:], q)
        return carry

    lax.fori_loop(0, n2 // cols, body, 0)


def _slab_rows(n2, na, a, b):
    return pl.multiple_of((b * na + a) * _slab_pitch(n2), SLAB_PAD)


def _fftconv_kernel(n1, n2, u_hbm, ks_hbm, m1_ref, m2b_ref, m2ib_ref, m3_ref, ta_ref, y_hbm,
                    a_ref, buf, kbuf, sem_in, sem_out, sem_k):
    h = n1 // 2
    pitch = n2 + ROW_PAD
    sp = _slab_pitch(n2)
    na = n1 // TW_B
    slabs = min(FFT_SLABS, na)
    groups = na // slabs
    cols = min(FFT_COLS, n2)
    npairs = pl.num_programs(1)
    total = pl.num_programs(0) * npairs
    t = pl.program_id(0) * npairs + pl.program_id(1)
    slot = t % 2

    def window(ref, tt):
        return ref.at[tt // npairs, pl.ds(2 * (tt % npairs), 2)]

    def in_copy(tt, s):
        return pltpu.make_async_copy(window(u_hbm, tt), buf.at[s], sem_in.at[s])

    def out_copy(tt, s):
        return pltpu.make_async_copy(buf.at[s], window(y_hbm, tt), sem_out.at[s])

    chunk = slabs * sp

    def k_copy(i, s):
        rows0 = pl.multiple_of(i * chunk, 8)
        return pltpu.make_async_copy(ks_hbm.at[pl.program_id(0), pl.ds(rows0, chunk)], kbuf.at[s], sem_k.at[s])

    k_copy(0, 0).start()

    @pl.when(t == 0)
    def _():
        in_copy(t, slot).start()

    in_copy(t, slot).wait()
    u_ref = buf.at[slot]

    def load_u(c):
        return jnp.concatenate([u_ref[0, pl.ds(c, h, stride=pitch), :],
                                u_ref[1, pl.ds(c, h, stride=pitch), :]], axis=0)

    _stage1(h, n1, n2, m1_ref, load_u, a_ref)

    @pl.when(t >= 1)
    def _():
        out_copy(t - 1, 1 - slot).wait()

    @pl.when(t + 1 < total)
    def _():
        in_copy(t + 1, 1 - slot).start()

    def slab_stage(i, carry):
        b = i // groups
        a0 = (i % groups) * slabs
        ks = i % 2
        k_copy(i, ks).wait()

        @pl.when(i + 1 < TW_B * groups)
        def _():
            k_copy(i + 1, 1 - ks).start()

        rows = [_slab_rows(n2, na, a0 + q, b) for q in range(slabs)]
        xs = [_cmul(a_ref[pl.ds(rows[q], 2 * n2), :], ta_ref[a0 + q]) for q in range(slabs)]
        f = _mm(m2b_ref[b], _lane_cat(xs))
        cs = [_cmul(_lane_part(f, q), kbuf[ks, pl.ds(q * sp, 2 * n2), :]) for q in range(slabs)]
        d = _mm(m2ib_ref[b], _lane_cat(cs))
        for q in range(slabs):
            a_ref[pl.ds(rows[q], 2 * n2), :] = _cmul(_lane_part(d, q), ta_ref[a0 + q], conj=True)
        return carry

    lax.fori_loop(0, TW_B * groups, slab_stage, 0)

    def stage3(i, carry):
        xs = []
        for q in range(cols):
            r = _col_rows(i * cols + q)
            xs.append(jnp.concatenate([a_ref[pl.ds(r, n1, stride=sp), :],
                                       a_ref[pl.ds(r + 8, n1, stride=sp), :]], axis=0))
        out = _mm(m3_ref[...], _lane_cat(xs))
        for q in range(cols):
            c = i * cols + q
            u_ref[0, pl.ds(c, h, stride=pitch), :] = _lane_part(out[:h], q)
            u_ref[1, pl.ds(c, h, stride=pitch), :] = _lane_part(out[h:], q)
        return carry

    lax.fori_loop(0, n2 // cols, stage3, 0)

    out_copy(t, slot).start()

    @pl.when(t == total - 1)
    def _():
        out_copy(t, slot).wait()


def _const_spec(a):
    return pl.BlockSpec(a.shape, lambda *_: tuple(0 for _ in a.shape), pipeline_mode=pl.Buffered(1))


def _fftconv(u, kspec, fc):
    ns, b, lp, _ = u.shape
    dh = ns * LANES
    n1, n2 = fc["n1"], fc["n2"]
    nslab = n1 * _slab_pitch(n2)
    consts = [fc["m1"], fc["m2b"], fc["m2ib"], fc["m3"], fc["ta"]]
    return pl.pallas_call(
        functools.partial(_fftconv_kernel, n1, n2),
        grid=(dh // LANES, b // 2),
        in_specs=[
            pl.BlockSpec(memory_space=pl.ANY),
            pl.BlockSpec(memory_space=pl.ANY),
        ] + [_const_spec(a) for a in consts],
        out_specs=pl.BlockSpec(memory_space=pl.ANY),
        out_shape=jax.ShapeDtypeStruct(u.shape, F32),
        scratch_shapes=[pltpu.VMEM((nslab, LANES), F32), pltpu.VMEM((2, 2, lp, LANES), F32),
                        pltpu.VMEM((2, min(FFT_SLABS, n1 // TW_B) * _slab_pitch(n2), LANES), F32),
                        pltpu.SemaphoreType.DMA((2,)), pltpu.SemaphoreType.DMA((2,)), pltpu.SemaphoreType.DMA((2,))],
        compiler_params=_params("arbitrary", "arbitrary"),
        name="fftconv",
    )(u, kspec, *consts)


def _filter_fft_kernel(n1, n2, k_ref, l1_ref, hyb_ref, m1_ref, m2b_ref, ta_ref, o_ref):
    scale = (1.0 / (n1 * n2)) / l1_ref[...]
    skip = hyb_ref[...] * (1.0 / (n1 * n2))
    pitch = n2 + ROW_PAD
    na = n1 // TW_B
    slabs = min(FFT_SLABS, na)
    groups = na // slabs

    _stage1(n1, n1, n2, m1_ref, lambda c: k_ref[pl.ds(c, n1, stride=pitch), :], o_ref)

    def slab_stage(i, carry):
        b = i // groups
        a0 = (i % groups) * slabs
        rows = [_slab_rows(n2, na, a0 + q, b) for q in range(slabs)]
        xs = [_cmul(o_ref[pl.ds(rows[q], 2 * n2), :], ta_ref[a0 + q]) for q in range(slabs)]
        f = _mm(m2b_ref[b], _lane_cat(xs))
        for q in range(slabs):
            fr, fi = _split_ri(_lane_part(f, q) * scale)
            o_ref[pl.ds(rows[q], 2 * n2), :] = _join_ri(fr + skip, fi)
            o_ref[pl.ds(rows[q] + 2 * n2, SLAB_PAD), :] = jnp.zeros((SLAB_PAD, LANES), F32)
        return carry

    lax.fori_loop(0, TW_B * groups, slab_stage, 0)


def _filter_fft(kern, l1, hy_bias, fc):
    npad, dh = kern.shape
    n1, n2 = fc["n1"], fc["n2"]
    nslab = n1 * _slab_pitch(n2)
    consts = [fc["m1_real"], fc["m2b"], fc["ta"]]
    return pl.pallas_call(
        functools.partial(_filter_fft_kernel, n1, n2),
        grid=(dh // LANES,),
        in_specs=[pl.BlockSpec((npad, LANES), lambda c: (0, c), pipeline_mode=pl.Buffered(1)),
                  pl.BlockSpec((1, LANES), lambda c: (0, c)),
                  pl.BlockSpec((1, LANES), lambda c: (0, c))]
        + [_const_spec(a) for a in consts],
        out_specs=pl.BlockSpec((None, nslab, LANES), lambda c: (c, 0, 0)),
        out_shape=jax.ShapeDtypeStruct((dh // LANES, nslab, LANES), F32),
        compiler_params=_params("parallel"),
        name="filter_fft",
    )(kern, l1, hy_bias.reshape(1, dh), *consts)


def _filter_mlp_kernel(z_ref, w1_ref, b1_ref, w2_ref, b2_ref, w3_ref, b3_ref, fr_ref, h_ref):
    dot = lambda a, b: jnp.dot(a, b, precision=HIGHEST, preferred_element_type=F32)
    h = jnp.sin(fr_ref[0:1, :] * (dot(z_ref[...], w1_ref[...]) + b1_ref[...]))
    h = jnp.sin(fr_ref[1:2, :] * (dot(h, w2_ref[...]) + b2_ref[...]))
    h_ref[...] = jnp.sin(fr_ref[2:3, :] * (dot(h, w3_ref[...]) + b3_ref[...]))


def _filter_out_kernel(l, tr, n2, h_ref, w4f_ref, w4b_ref, dl_ref, o_ref, l1_ref):
    rc = pl.program_id(1)
    hcat = h_ref[...]
    hf = jnp.dot(hcat, w4f_ref[...], precision=HIGHEST, preferred_element_type=F32)
    hb = jnp.dot(hcat, w4b_ref[...], precision=HIGHEST, preferred_element_type=F32)
    row = lax.broadcasted_iota(jnp.int32, (tr, LANES), 0) + rc * tr
    step = 1.0 / (l - 1)
    t_f = row.astype(F32) * step
    t_b = (l - row).astype(F32) * step
    kf = hf * jnp.exp(-t_f * dl_ref[...])
    kb = jnp.where(row == 0, 0.0, hb * jnp.exp(-t_b * dl_ref[...]))
    for half, val in ((0, kf), (1, kb)):
        def store(r, n, v, half=half):
            o_ref[half, r:r + n, :] = v
        _store_pitched(store, val, n2)
    part = jnp.sum(jnp.abs(kf), axis=0, keepdims=True) + jnp.sum(jnp.abs(kb), axis=0, keepdims=True)

    @pl.when(rc == 0)
    def _():
        l1_ref[...] = part

    @pl.when(rc > 0)
    def _():
        l1_ref[...] += part


def _filter_feats(l, bands):
    f32 = np.float32
    t = np.linspace(0.0, 1.0, l, dtype=f32)[:, None]
    w = (f32(2.0 * math.pi) * np.arange(l, dtype=f32)[:, None] / f32(l)).astype(f32)
    f = np.linspace(1e-4, bands - 1, bands, dtype=f32)[None, :]
    z = np.concatenate([t, np.cos(f * w), -np.sin(f * w)], axis=-1).astype(f32)
    z_rev = np.concatenate([z[:1], z[1:][::-1]], axis=0)
    return jnp.asarray(np.concatenate([z, z_rev], axis=-1))


def _hyena_filter(l, n2, f_w1, f_b1, f_w2, f_b2, f_w3, f_b3, f_w4, f_freq):
    emb, hid = f_w1.shape
    dh = f_w4.shape[1] // 2
    tr = min(l, 1024)
    z2 = _filter_feats(l, (emb - 1) // 2)
    bd = lambda w: jnp.kron(jnp.eye(2, dtype=F32), w)
    two = lambda v: jnp.concatenate([v, v], axis=-1)
    margs = [z2, bd(f_w1), two(f_b1)[None], bd(f_w2), two(f_b2)[None], bd(f_w3), two(f_b3)[None], two(f_freq)]
    full1 = lambda a: pl.BlockSpec(a.shape, lambda r: tuple(0 for _ in a.shape))
    hcat = pl.pallas_call(
        _filter_mlp_kernel,
        grid=(l // tr,),
        in_specs=[pl.BlockSpec((tr, 2 * emb), lambda r: (r, 0))] + [full1(a) for a in margs[1:]],
        out_specs=pl.BlockSpec((tr, 2 * hid), lambda r: (r, 0)),
        out_shape=jax.ShapeDtypeStruct((l, 2 * hid), F32),
        compiler_params=_params("parallel"),
        name="filter_mlp",
    )(*margs)
    zeros = jnp.zeros((hid, dh), F32)
    w4f = jnp.concatenate([f_w4[:, :dh], zeros], axis=0)
    w4b = jnp.concatenate([zeros, f_w4[:, dh:]], axis=0)
    max_decay = math.log(FILTER_TARGET) / FAST_DECAY
    min_decay = math.log(FILTER_TARGET) / SLOW_DECAY
    deltas = jnp.abs(jnp.linspace(min_decay, max_decay, dh, dtype=F32))[None, :]
    kern, l1 = pl.pallas_call(
        functools.partial(_filter_out_kernel, l, tr, n2),
        grid=(dh // LANES, l // tr),
        in_specs=[
            pl.BlockSpec((tr, 2 * hid), lambda c, r: (r, 0)),
            pl.BlockSpec((2 * hid, LANES), lambda c, r: (0, c)),
            pl.BlockSpec((2 * hid, LANES), lambda c, r: (0, c)),
            pl.BlockSpec((1, LANES), lambda c, r: (0, c)),
        ],
        out_specs=[pl.BlockSpec((2, _pitched_rows(tr, n2), LANES), lambda c, r: (0, r, c)),
                   pl.BlockSpec((1, LANES), lambda c, r: (0, c))],
        out_shape=[jax.ShapeDtypeStruct((2, _pitched_rows(l, n2), dh), F32), jax.ShapeDtypeStruct((1, dh), F32)],
        compiler_params=_params("parallel", "arbitrary"),
        name="filter_out",
    )(hcat, w4f, w4b, deltas)
    return kern.reshape(2 * _pitched_rows(l, n2), dh), l1


def _grid_sincos(rows, dim):
    r, col = jnp.meshgrid(jnp.arange(rows, dtype=F32), jnp.arange(GRID_W, dtype=F32), indexing='ij')
    quarter = dim // 4
    omega = 1.0 / (10000.0 ** (jnp.arange(quarter, dtype=F32) / quarter))

    def emb(pos):
        ang = pos.reshape(-1, 1) * omega[None, :]
        return jnp.concatenate([jnp.sin(ang), jnp.cos(ang)], axis=-1)

    return jnp.concatenate([emb(r), emb(col)], axis=-1)


def kernel(x, c, ctx, c_ctx, w_mod, b_mod, norm_g, w_in, hy_conv_w, hy_conv_b, f_w1, f_b1, f_w2, f_b2,
           f_w3, f_b3, f_w4, f_freq, hy_bias, rg_conv_w, rg_conv_b, rg_wa, rg_ba, rg_wx, rg_bx, rg_lam,
           br_norm_h, br_norm_r, w_out, final_g):
    B, N, D = x.shape
    L_ctx = ctx.shape[1]
    depth = w_mod.shape[0]
    dh = hy_bias.shape[1]
    dr = rg_conv_b.shape[1]
    assert B == NB and B % BB == 0 and dh == dr

    pos = _grid_sincos(N // GRID_W, D)
    cond = jnp.concatenate([c, jnp.broadcast_to(c_ctx[None], (8, D))], axis=0)
    mod = _modulation(cond, w_mod, b_mod)
    w_in_bf = w_in.astype(BF16)
    w_out_bf = w_out.astype(BF16)
    fc_lat = _fft_consts(N)
    fc_ctx = _fft_consts(L_ctx)

    n2_l, n2_c = fc_lat["n2"], fc_ctx["n2"]
    state = lambda h, rows: h[:, rows, :].transpose(1, 0, 2).reshape(NB, dr)

    xl, xc = x, ctx
    for l in range(depth):
        last = l == depth - 1
        ml = mod[l, :B].reshape(B, 1, 3 * D)
        mc = jnp.broadcast_to(mod[l, B:B + 1].reshape(1, 1, 3 * D), (B, 1, 3 * D))
        sh_l, sc_l, g_l = ml[..., :D], ml[..., D:2 * D], ml[..., 2 * D:]
        sh_c, sc_c, g_c = mc[..., :D], mc[..., D:2 * D], mc[..., 2 * D:]
        proj = (norm_g[l], w_in_bf[l], hy_conv_w[l], hy_conv_b[l], dh, dr)
        if l == 0:
            u_l, x0_l, zh_l, zr_l, xr_l, xl = _inproj(xl, pos, sh_l, sc_l, *proj, n2_l)
        else:
            u_l, x0_l, zh_l, zr_l, xr_l = _inproj(xl, None, sh_l, sc_l, *proj, n2_l)
        u_c, x0_c, zh_c, zr_c, xr_c = _inproj(xc, None, sh_c, sc_c, *proj, n2_c)
        rnn_w = _rglru_weights(rg_wa[l], rg_ba[l], rg_wx[l], rg_bx[l])
        zero = jnp.zeros((NB, dr), F32)
        hf_c, hb_c = _rglru(xr_c, rg_conv_w[l], rg_conv_b[l], *rnn_w, rg_lam[l], zero, zero)
        hf_l, hb_l = _rglru(xr_l, rg_conv_w[l], rg_conv_b[l], *rnn_w, rg_lam[l],
                            state(hf_c, slice(-NB, None)), state(hb_c, slice(0, NB)))
        filt = (f_w1[l], f_b1[l], f_w2[l], f_b2[l], f_w3[l], f_b3[l], f_w4[l], f_freq[l])
        shared = (br_norm_h[l], br_norm_r[l], w_out_bf[l])
        cv_l = _fftconv(u_l, _filter_fft(*_hyena_filter(N, n2_l, *filt), hy_bias[l], fc_lat), fc_lat)
        if not last:
            cv_c = _fftconv(u_c, _filter_fft(*_hyena_filter(L_ctx, n2_c, *filt), hy_bias[l], fc_ctx), fc_ctx)
            xc = _outproj(cv_c, x0_c, n2_c, hf_c, hb_c, zh_c, zr_c, *shared, xc, g_c, None)
        xl = _outproj(cv_l, x0_l, n2_l, hf_l, hb_l, zh_l, zr_l, *shared, xl, g_l,
                      final_g if last else None)
    return xl
```

```python
import functools
import math

import jax
import jax.numpy as jnp
import numpy as np
from jax import lax
from jax.experimental import pallas as pl
from jax.experimental.pallas import tpu as pltpu

GRID_W = 64
FILTER_TARGET = 1e-2
FAST_DECAY = 0.3
SLOW_DECAY = 1.5
RG_C = 8.0
EPS = 1e-6

LANES = 128
F32 = jnp.float32
BF16 = jnp.bfloat16
HIGHEST = lax.Precision.HIGHEST

VMEM_LIMIT_BYTES = 56 * 1024 * 1024


def _params(*sem):
    return pltpu.CompilerParams(dimension_semantics=sem, vmem_limit_bytes=VMEM_LIMIT_BYTES)


def _silu(x):
    return x * jax.nn.sigmoid(x)


def _rms(x, g):
    return x * lax.rsqrt(jnp.mean(x * x, axis=-1, keepdims=True) + EPS) * g


def _mod_kernel(s_ref, w_ref, b_ref, o_ref):
    s = _silu(s_ref[...])
    o_ref[0] = jnp.dot(s, w_ref[0], precision=HIGHEST, preferred_element_type=F32) + b_ref[0]


def _modulation(cond, w_mod, b_mod):
    depth, d, d3 = w_mod.shape
    rows = cond.shape[0]
    tn = d // 2
    return pl.pallas_call(
        _mod_kernel,
        grid=(depth, d3 // tn),
        in_specs=[
            pl.BlockSpec((rows, d), lambda l, j: (0, 0)),
            pl.BlockSpec((1, d, tn), lambda l, j: (l, 0, j)),
            pl.BlockSpec((1, 1, tn), lambda l, j: (l, 0, j)),
        ],
        out_specs=pl.BlockSpec((1, rows, tn), lambda l, j: (l, 0, j)),
        out_shape=jax.ShapeDtypeStruct((depth, rows, d3), F32),
        compiler_params=_params("parallel", "parallel"),
        name="modulation",
    )(cond, w_mod, b_mod.reshape(depth, 1, d3))


NB = 8
HALO = 8
BB = 4
GROUP = 2


def _inproj_kernel(add_pos, tm, dh, dr, n2, *refs):
    refs = list(refs)
    x_ref, xp_ref, xn_ref = refs[:3]
    refs = refs[3:]
    if add_pos:
        pos_ref, posp_ref, posn_ref = refs[:3]
        refs = refs[3:]
    (sh_ref, sc_ref, g_ref, wh_ref, wr_ref, cw_ref, cb_ref, u_ref, x0_ref, zh_ref, zr_ref, xr_ref) = refs[:12]
    j = pl.program_id(0)
    i = pl.program_id(1)
    keep_prev = (j > 0).astype(F32)
    keep_next = (j < pl.num_programs(0) - 1).astype(F32)
    seg = tm + 2 * HALO

    for b0 in range(0, BB, GROUP):
        xs = []
        for b in range(b0, b0 + GROUP):
            xb = jnp.concatenate([xp_ref[b], x_ref[b], xn_ref[b]], axis=0)
            if add_pos:
                xb = xb + jnp.concatenate([posp_ref[...], pos_ref[...], posn_ref[...]], axis=0)
                refs[12][b] = xb[HALO:HALO + tm]
            xs.append((_rms(xb, g_ref[...]) * (1.0 + sc_ref[b]) + sh_ref[b]).astype(BF16))

        ph = jnp.dot(jnp.concatenate(xs, axis=0), wh_ref[...], preferred_element_type=F32)
        for k, b in enumerate(range(b0, b0 + GROUP)):
            pb = ph[k * seg:(k + 1) * seg]
            full = jnp.concatenate([pb[:HALO] * keep_prev, pb[HALO:HALO + tm], pb[HALO + tm:] * keep_next], axis=0)
            c = cb_ref[...] + cw_ref[0:1, :] * full[HALO - 1:HALO - 1 + tm]
            c = c + cw_ref[1:2, :] * full[HALO:HALO + tm]
            c = c + cw_ref[2:3, :] * full[HALO + 1:HALO + 1 + tm]
            x0_ref[b] = c[:, :dh].astype(x0_ref.dtype)

            def store(r, n, v, b=b):
                for s_ in range(dh // LANES):
                    u_ref[s_, b, r:r + n, :] = _lane_part(v, s_)
            _store_pitched(store, c[:, 2 * dh:] * c[:, dh:2 * dh], n2)

        xn_mid = jnp.concatenate([x[HALO:HALO + tm] for x in xs], axis=0)
        pr = jnp.dot(xn_mid, wr_ref[...], preferred_element_type=F32)
        for k, b in enumerate(range(b0, b0 + GROUP)):
            pb = pr[k * tm:(k + 1) * tm]
            zh_ref[b] = pb[:, :dh].astype(zh_ref.dtype)
            zr_ref[b] = pb[:, dh + dr:].astype(zr_ref.dtype)
            for s_ in range(dr // LANES):
                xr_ref[s_, pl.ds(i * BB + b, tm, stride=NB), :] = pb[:, dh + s_ * LANES:dh + (s_ + 1) * LANES]


def _inproj(x, pos, sh, sc, g, w_bf16, conv_w, conv_b, dh, dr, n2):
    b, l, d = x.shape
    tm = min(l, 128)
    add_pos = pos is not None
    nt = l // HALO
    per = tm // HALO
    row = lambda w: pl.BlockSpec((BB, tm, w), lambda j, i: (i, j, 0))
    prev = lambda j: jnp.maximum(j * per - 1, 0)
    nxt = lambda j: jnp.minimum((j + 1) * per, nt - 1)
    vec = pl.BlockSpec((BB, 1, d), lambda j, i: (i, 0, 0))
    full = lambda a: pl.BlockSpec(a.shape, lambda j, i: tuple(0 for _ in a.shape), pipeline_mode=pl.Buffered(1))
    in_specs = [row(d),
                pl.BlockSpec((BB, HALO, d), lambda j, i: (i, prev(j), 0)),
                pl.BlockSpec((BB, HALO, d), lambda j, i: (i, nxt(j), 0))]
    args = [x, x, x]
    if add_pos:
        in_specs += [pl.BlockSpec((tm, d), lambda j, i: (j, 0)),
                     pl.BlockSpec((HALO, d), lambda j, i: (prev(j), 0)),
                     pl.BlockSpec((HALO, d), lambda j, i: (nxt(j), 0))]
        args += [pos, pos, pos]
    w_h = w_bf16[:, :3 * dh]
    w_r = w_bf16[:, 3 * dh:]
    consts = [g.reshape(1, d), w_h, w_r, conv_w, conv_b.reshape(1, 3 * dh)]
    in_specs += [vec, vec] + [full(a) for a in consts]
    args += [sh, sc] + consts
    out_specs = [pl.BlockSpec((dh // LANES, BB, _pitched_rows(tm, n2), LANES), lambda j, i: (0, i, j, 0)),
                 row(dh), row(dh), row(dr),
                 pl.BlockSpec((dr // LANES, tm * NB, LANES), lambda j, i: (0, j, 0))]
    out_shape = [jax.ShapeDtypeStruct((dh // LANES, b, _pitched_rows(l, n2), LANES), F32),
                 jax.ShapeDtypeStruct((b, l, dh), BF16), jax.ShapeDtypeStruct((b, l, dh), BF16),
                 jax.ShapeDtypeStruct((b, l, dr), BF16),
                 jax.ShapeDtypeStruct((dr // LANES, l * NB, LANES), F32)]
    if add_pos:
        out_specs.append(row(d))
        out_shape.append(jax.ShapeDtypeStruct((b, l, d), F32))
    return pl.pallas_call(
        functools.partial(_inproj_kernel, add_pos, tm, dh, dr, n2),
        grid=(l // tm, b // BB),
        in_specs=in_specs,
        out_specs=out_specs,
        out_shape=out_shape,
        compiler_params=_params("arbitrary", "arbitrary"),
        name="inproj_pos" if add_pos else "inproj",
    )(*args)


def _outproj_kernel(final, tm, n2, *refs):
    if final:
        (cv_ref, x0_ref, hf_ref, hb_ref, zh_ref, zr_ref, gnh_ref, gnr_ref, wt_ref, wb_ref,
         x_ref, gate_ref, fg_ref, o_ref) = refs
    else:
        (cv_ref, x0_ref, hf_ref, hb_ref, zh_ref, zr_ref, gnh_ref, gnr_ref, wt_ref, wb_ref,
         x_ref, gate_ref, o_ref) = refs
    i = pl.program_id(1)
    yhs, yrs = [], []
    for b in range(BB):
        cv = _lane_cat([_load_pitched(lambda r, n, b=b, s_=s_: cv_ref[s_, b, r:r + n, :], tm, n2)
                        for s_ in range(cv_ref.shape[0])])
        y_h = cv * x0_ref[b].astype(F32)
        yhs.append((_rms(y_h, gnh_ref[...]) * _silu(zh_ref[b].astype(F32))).astype(BF16))
        hs = _lane_cat([hf_ref[s_, pl.ds(i * BB + b, tm, stride=NB), :] + hb_ref[s_, pl.ds(i * BB + b, tm, stride=NB), :]
                        for s_ in range(hf_ref.shape[0])])
        yrs.append((_rms(hs, gnr_ref[...]) * _silu(zr_ref[b].astype(F32))).astype(BF16))
    y = jnp.dot(jnp.concatenate(yhs, axis=0), wt_ref[...], preferred_element_type=F32)
    y = y + jnp.dot(jnp.concatenate(yrs, axis=0), wb_ref[...], preferred_element_type=F32)
    for b in range(BB):
        out = x_ref[b] + gate_ref[b] * y[b * tm:(b + 1) * tm]
        if final:
            out = _rms(out, fg_ref[...])
        o_ref[b] = out


def _outproj(conv, x0, n2, hf, hb, zh, zr, gn_h, gn_r, w_out_bf16, x, gate, final_g):
    b, l, d = x.shape
    dh = x0.shape[-1]
    dr = zr.shape[-1]
    tm = min(l, 128)
    final = final_g is not None
    row = lambda w: pl.BlockSpec((BB, tm, w), lambda j, i: (i, j, 0))
    pitched = pl.BlockSpec((dh // LANES, BB, _pitched_rows(tm, n2), LANES), lambda j, i: (0, i, j, 0))
    slabs = pl.BlockSpec((dr // LANES, tm * NB, LANES), lambda j, i: (0, j, 0))
    full = lambda a: pl.BlockSpec(a.shape, lambda j, i: tuple(0 for _ in a.shape), pipeline_mode=pl.Buffered(1))
    consts = [gn_h.reshape(1, dh), gn_r.reshape(1, dr), w_out_bf16[:dh], w_out_bf16[dh:]]
    in_specs = [pitched, row(dh), slabs, slabs, row(dh), row(dr)] + [full(a) for a in consts] + [
        row(d), pl.BlockSpec((BB, 1, d), lambda j, i: (i, 0, 0))]
    args = [conv, x0, hf, hb, zh, zr] + consts + [x, gate]
    if final:
        fg = final_g.reshape(1, d)
        in_specs.append(full(fg))
        args.append(fg)
    return pl.pallas_call(
        functools.partial(_outproj_kernel, final, tm, n2),
        grid=(l // tm, b // BB),
        in_specs=in_specs,
        out_specs=row(d),
        out_shape=jax.ShapeDtypeStruct((b, l, d), F32),
        compiler_params=_params("parallel", "parallel"),
        name="outproj_final" if final else "outproj",
    )(*args)


RG_PREV = 2
RG_NEXT = 1
SQRT_FLOOR = 1e-30
LOG2E = math.log2(math.e)


def _softplus(x):
    return jnp.maximum(x, 0.0) + jnp.log(1.0 + jnp.exp(-jnp.abs(x)))


def _sqrt01(y):
    return y * lax.rsqrt(jnp.maximum(y, SQRT_FLOOR))


def _rglru_kernel(tc, xf_ref, xfp_ref, xfn_ref, xb_ref, xbp_ref, xbn_ref, cw_ref, cb_ref,
                  wf_ref, bf_ref, wb_ref, bb_ref, lam_ref, h0f_ref, h0b_ref,
                  hf_ref, hb_ref, af_s, bf_s, ab_s, bb_s, cf_s, cbk_s):
    j = pl.program_id(0)
    nj = pl.num_programs(0)
    rows = tc * NB
    ns = xf_ref.shape[0]
    dr = ns * LANES
    wide = lambda ref: _lane_cat([ref[s_] for s_ in range(ns)])

    @pl.when(j == 0)
    def _():
        cf_s[...] = h0f_ref[...]
        cbk_s[...] = h0b_ref[...]

    def coeffs(x_ref, xp_ref, xn_ref, first, last, w_ref, b_ref, lam, a_s, b_s):
        prev = jnp.where(first, 0.0, wide(xp_ref))
        nxt = jnp.where(last, 0.0, wide(xn_ref))
        xfull = jnp.concatenate([prev, wide(x_ref), nxt], axis=0)
        xh = cb_ref[...] + cw_ref[0:1, :] * xfull[0:rows]
        for k in range(1, RG_PREV + RG_NEXT + 1):
            xh = xh + cw_ref[k:k + 1, :] * xfull[k * NB:k * NB + rows]
        g = jnp.dot(xh.astype(BF16), w_ref[...], preferred_element_type=F32) + b_ref[...]
        t_r = jnp.tanh(g[:, :dr])
        t_i = jnp.tanh(g[:, dr:])
        c1 = (-0.5 * RG_C * LOG2E) * _softplus(-lam)
        a = jnp.exp2(c1 * t_r + c1)
        a_s[...] = a
        b_s[...] = _sqrt01(1.0 - a * a) * (t_i * xh + xh)

    coeffs(xf_ref, xfp_ref, xfn_ref, j == 0, j == nj - 1, wf_ref, bf_ref, lam_ref[0:1, :], af_s, bf_s)
    coeffs(xb_ref, xbp_ref, xbn_ref, j == nj - 1, j == 0, wb_ref, bb_ref, lam_ref[1:2, :], ab_s, bb_s)

    def step(t, carry):
        hf, hb = carry
        rf = pl.multiple_of(t * NB, NB)
        rb = pl.multiple_of((tc - 1 - t) * NB, NB)
        hf = af_s[pl.ds(rf, NB), :] * hf + bf_s[pl.ds(rf, NB), :]
        hb = ab_s[pl.ds(rb, NB), :] * hb + bb_s[pl.ds(rb, NB), :]
        for s_ in range(ns):
            hf_ref[s_, pl.ds(rf, NB), :] = _lane_part(hf, s_)
            hb_ref[s_, pl.ds(rb, NB), :] = _lane_part(hb, s_)
        return hf, hb

    hf, hb = lax.fori_loop(0, tc, step, (cf_s[...], cbk_s[...]), unroll=8)
    cf_s[...] = hf
    cbk_s[...] = hb


def _rglru(xr2, conv_w, conv_b, w_f, b_f, w_b, b_b, lam, h0_f, h0_b):
    ns, rows_total, _ = xr2.shape
    dr = ns * LANES
    l = rows_total // NB
    tc = min(l, 128)
    nj = l // tc
    rows = tc * NB
    prev_rows = RG_PREV * NB
    last_prev = rows // prev_rows
    n_next = rows_total // NB

    def blk(shape, fn):
        return pl.BlockSpec(shape, fn)

    full = lambda shape: pl.BlockSpec(shape, lambda j: tuple(0 for _ in shape))
    in_specs = [
        blk((ns, rows, LANES), lambda j: (0, j, 0)),
        blk((ns, prev_rows, LANES), lambda j: (0, jnp.maximum(j * last_prev - 1, 0), 0)),
        blk((ns, NB, LANES), lambda j: (0, jnp.minimum((j + 1) * tc, n_next - 1), 0)),
        blk((ns, rows, LANES), lambda j: (0, nj - 1 - j, 0)),
        blk((ns, prev_rows, LANES), lambda j: (0, jnp.maximum((nj - 1 - j) * last_prev - 1, 0), 0)),
        blk((ns, NB, LANES), lambda j: (0, jnp.minimum((nj - j) * tc, n_next - 1), 0)),
        full(conv_w.shape), full((1, dr)),
        full(w_f.shape), full((1, 2 * dr)), full(w_b.shape), full((1, 2 * dr)),
        full(lam.shape), full((NB, dr)), full((NB, dr)),
    ]
    out_specs = [blk((ns, rows, LANES), lambda j: (0, j, 0)), blk((ns, rows, LANES), lambda j: (0, nj - 1 - j, 0))]
    out_shape = [jax.ShapeDtypeStruct((ns, rows_total, LANES), F32)] * 2
    scratch = [pltpu.VMEM((rows, dr), F32)] * 4 + [pltpu.VMEM((NB, dr), F32)] * 2
    return pl.pallas_call(
        functools.partial(_rglru_kernel, tc),
        grid=(nj,),
        in_specs=in_specs,
        out_specs=out_specs,
        out_shape=out_shape,
        scratch_shapes=scratch,
        compiler_params=_params("arbitrary"),
        name="rglru",
    )(xr2, xr2, xr2, xr2, xr2, xr2, 0.5 * conv_w, 0.5 * conv_b.reshape(1, dr), w_f, 0.5 * b_f, w_b, 0.5 * b_b,
      lam, h0_f, h0_b)


def _block_diag(w):
    heads, blk, _ = w.shape
    eye = jnp.eye(heads, dtype=w.dtype)
    return jnp.einsum('hij,hg->higj', w, eye).reshape(heads * blk, heads * blk)


def _rglru_weights(wa, ba, wx, bx):
    out = []
    for d in range(2):
        w = jnp.concatenate([_block_diag(wa[d]), _block_diag(wx[d])], axis=1).astype(BF16)
        b = jnp.concatenate([ba[d], bx[d]])[None, :]
        out += [w, b]
    return out


TW_B = 8
ROW_PAD = 8
SLAB_PAD = 4
MIN_STREAM_BYTES = 1 << 20
FFT_COLS = 16
FFT_SLABS = 16


def _fft_dims(l):
    n = 2 * l
    n1 = 16
    while (2 * n1) * (2 * n1) <= n:
        n1 *= 2
    return n1, n // n1


def _pitched_rows(rows, n2):
    return rows // n2 * (n2 + ROW_PAD)


def _store_pitched(store, val, n2):
    p = n2 + ROW_PAD
    pad = jnp.zeros((ROW_PAD, val.shape[1]), val.dtype)
    for g in range(val.shape[0] // n2):
        store(g * p, n2, val[g * n2:(g + 1) * n2])
        store(g * p + n2, ROW_PAD, pad)


def _load_pitched(load, rows, n2):
    p = n2 + ROW_PAD
    return jnp.concatenate([load(g * p, n2) for g in range(rows // n2)], axis=0)


def _slab_pitch(n2):
    return 2 * n2 + SLAB_PAD


def _fft_consts(l):
    n1, n2 = _fft_dims(l)
    n = n1 * n2
    h = n1 // 2
    f1 = np.exp(-2j * np.pi * np.outer(np.arange(n1), np.arange(n1)) / n1)
    f2 = np.exp(-2j * np.pi * np.outer(np.arange(n2), np.arange(n2)) / n2)
    stack = lambda a: np.block([[a.real, -a.imag], [a.imag, a.real]])
    bf = lambda a: jnp.asarray(a, F32).astype(BF16)

    g, r = np.divmod(np.arange(2 * n2), 16)
    perm = np.where(r < 8, 8 * g + r, n2 + 8 * g + (r - 8))
    inter = lambda a: a[perm][:, perm]

    m1_full = stack(f1)
    m1_half = np.concatenate([m1_full[:, :h], m1_full[:, n1:n1 + h]], axis=1)
    m3_full = stack(np.conj(f1))
    m3 = np.concatenate([m3_full[:h], m3_full[n1:n1 + h]], axis=0)
    na = n1 // TW_B
    p = np.arange(n1)
    k1_of_p = TW_B * (p % na) + p // na
    order = np.concatenate([k1_of_p, n1 + k1_of_p])
    m1_half, m1_real, m3 = m1_half[order], m1_full[:, :n1][order], m3[:, order]
    ang_a = -2.0 * np.pi * np.outer(np.arange(na) * TW_B, np.arange(n2)) / n
    ang_b = -2.0 * np.pi * np.outer(np.arange(TW_B), np.arange(n2)) / n
    ta = np.concatenate([np.cos(ang_a), np.sin(ang_a)], axis=1)[:, perm]
    ta = jnp.asarray(np.broadcast_to(ta[..., None], ta.shape + (LANES,)), F32)
    tw_b = np.exp(1j * ang_b)
    fwd = [inter(stack(f2 * tw_b[b][None, :])) for b in range(TW_B)]
    inv = [inter(stack(np.conj(f2) * np.conj(tw_b[b])[:, None])) for b in range(TW_B)]
    return dict(n1=n1, n2=n2, m1=bf(m1_half), m3=bf(m3), ta=ta,
                m2b=jnp.stack([bf(m) for m in fwd]), m2ib=jnp.stack([bf(m) for m in inv]),
                m1_real=bf(m1_real))


def _mm(m, x):
    return jnp.dot(m, x.astype(BF16), preferred_element_type=F32)


def _lane_cat(parts):
    return parts[0] if len(parts) == 1 else jnp.concatenate(parts, axis=1)


def _lane_part(x, i):
    return x[:, i * LANES:(i + 1) * LANES]


def _split_ri(x):
    x4 = x.reshape(x.shape[0] // 16, 2, 8, x.shape[1])
    return x4[:, 0], x4[:, 1]


def _join_ri(re, im):
    return jnp.stack([re, im], axis=1).reshape(2 * re.shape[0] * 8, re.shape[2])


def _cmul(x, t, conj=False):
    xr, xi = _split_ri(x)
    tr, ti = _split_ri(t)
    if conj:
        return _join_ri(xr * tr + xi * ti, xi * tr - xr * ti)
    return _join_ri(xr * tr - xi * ti, xr * ti + xi * tr)


def _col_rows(c):
    return (c // 8) * 16 + c % 8


def _stage1(n_in, n1, n2, m_ref, load, a_ref):
    cols = min(FFT_COLS, n2)
    sp = _slab_pitch(n2)

    def body(i, carry):
        out = _mm(m_ref[...], _lane_cat([load(i * cols + q) for q in range(cols)]))
        for q in range(cols):
            r = _col_rows(i * cols + q)
            a_ref[pl.ds(r, n1, stride=sp), :] = _lane_part(out[:n1], q)
            a_ref[pl.ds(r + 8, n1, stride=sp), :] = _lane_part(out[n1:], q)
        return carry

    lax.fori_loop(0, n2 // cols, body, 0)


def _slab_rows(n2, na, a, b):
    return pl.multiple_of((b * na + a) * _slab_pitch(n2), SLAB_PAD)


def _fftconv_kernel(n1, n2, u_hbm, ks_hbm, m1_ref, m2b_ref, m2ib_ref, m3_ref, ta_ref, y_hbm,
                    a_ref, buf, kbuf, sem_in, sem_out, sem_k):
    h = n1 // 2
    pitch = n2 + ROW_PAD
    sp = _slab_pitch(n2)
    na = n1 // TW_B
    slabs = min(FFT_SLABS, na)
    groups = na // slabs
    cols = min(FFT_COLS, n2)
    npairs = pl.num_programs(1)
    total = pl.num_programs(0) * npairs
    t = pl.program_id(0) * npairs + pl.program_id(1)
    slot = t % 2

    def window(ref, tt):
        return ref.at[tt // npairs, pl.ds(2 * (tt % npairs), 2)]

    def in_copy(tt, s):
        return pltpu.make_async_copy(window(u_hbm, tt), buf.at[s], sem_in.at[s])

    def out_copy(tt, s):
        return pltpu.make_async_copy(buf.at[s], window(y_hbm, tt), sem_out.at[s])

    nchunks = TW_B * groups if slabs * sp * LANES * 4 >= MIN_STREAM_BYTES else 1
    chunk = (n1 * sp) // nchunks
    first_pair = pl.program_id(1) == 0

    def k_copy(i):
        rows0 = pl.multiple_of(i * chunk, 8)
        return pltpu.make_async_copy(ks_hbm.at[pl.program_id(0), pl.ds(rows0, chunk)],
                                     kbuf.at[pl.ds(rows0, chunk)], sem_k.at[i % 2])

    @pl.when(first_pair)
    def _():
        k_copy(0).start()

    @pl.when(t == 0)
    def _():
        in_copy(t, slot).start()

    in_copy(t, slot).wait()
    u_ref = buf.at[slot]

    def load_u(c):
        return jnp.concatenate([u_ref[0, pl.ds(c, h, stride=pitch), :],
                                u_ref[1, pl.ds(c, h, stride=pitch), :]], axis=0)

    _stage1(h, n1, n2, m1_ref, load_u, a_ref)

    @pl.when(t >= 1)
    def _():
        out_copy(t - 1, 1 - slot).wait()

    @pl.when(t + 1 < total)
    def _():
        in_copy(t + 1, 1 - slot).start()

    if nchunks == 1:
        @pl.when(first_pair)
        def _():
            k_copy(0).wait()

    def slab_stage(i, carry):
        b = i // groups
        a0 = (i % groups) * slabs
        if nchunks > 1:
            @pl.when(first_pair)
            def _():
                k_copy(i).wait()

            @pl.when(first_pair & (i + 1 < nchunks))
            def _():
                k_copy(i + 1).start()

        rows = [_slab_rows(n2, na, a0 + q, b) for q in range(slabs)]
        xs = [_cmul(a_ref[pl.ds(rows[q], 2 * n2), :], ta_ref[a0 + q]) for q in range(slabs)]
        f = _mm(m2b_ref[b], _lane_cat(xs))
        cs = [_cmul(_lane_part(f, q), kbuf[pl.ds(rows[q], 2 * n2), :]) for q in range(slabs)]
        d = _mm(m2ib_ref[b], _lane_cat(cs))
        for q in range(slabs):
            a_ref[pl.ds(rows[q], 2 * n2), :] = _cmul(_lane_part(d, q), ta_ref[a0 + q], conj=True)
        return carry

    lax.fori_loop(0, TW_B * groups, slab_stage, 0)

    def stage3(i, carry):
        xs = []
        for q in range(cols):
            r = _col_rows(i * cols + q)
            xs.append(jnp.concatenate([a_ref[pl.ds(r, n1, stride=sp), :],
                                       a_ref[pl.ds(r + 8, n1, stride=sp), :]], axis=0))
        out = _mm(m3_ref[...], _lane_cat(xs))
        for q in range(cols):
            c = i * cols + q
            u_ref[0, pl.ds(c, h, stride=pitch), :] = _lane_part(out[:h], q)
            u_ref[1, pl.ds(c, h, stride=pitch), :] = _lane_part(out[h:], q)
        return carry

    lax.fori_loop(0, n2 // cols, stage3, 0)

    out_copy(t, slot).start()

    @pl.when(t == total - 1)
    def _():
        out_copy(t, slot).wait()


def _const_spec(a):
    return pl.BlockSpec(a.shape, lambda *_: tuple(0 for _ in a.shape), pipeline_mode=pl.Buffered(1))


def _fftconv(u, kspec, fc):
    ns, b, lp, _ = u.shape
    dh = ns * LANES
    n1, n2 = fc["n1"], fc["n2"]
    nslab = n1 * _slab_pitch(n2)
    consts = [fc["m1"], fc["m2b"], fc["m2ib"], fc["m3"], fc["ta"]]
    return pl.pallas_call(
        functools.partial(_fftconv_kernel, n1, n2),
        grid=(dh // LANES, b // 2),
        in_specs=[
            pl.BlockSpec(memory_space=pl.ANY),
            pl.BlockSpec(memory_space=pl.ANY),
        ] + [_const_spec(a) for a in consts],
        out_specs=pl.BlockSpec(memory_space=pl.ANY),
        out_shape=jax.ShapeDtypeStruct(u.shape, F32),
        scratch_shapes=[pltpu.VMEM((nslab, LANES), F32), pltpu.VMEM((2, 2, lp, LANES), F32),
                        pltpu.VMEM((nslab, LANES), F32),
                        pltpu.SemaphoreType.DMA((2,)), pltpu.SemaphoreType.DMA((2,)), pltpu.SemaphoreType.DMA((2,))],
        compiler_params=_params("arbitrary", "arbitrary"),
        name="fftconv",
    )(u, kspec, *consts)


def _filter_fft_kernel(n1, n2, k_ref, l1_ref, hyb_ref, m1_ref, m2b_ref, ta_ref, o_ref):
    scale = (1.0 / (n1 * n2)) / l1_ref[...]
    skip = hyb_ref[...] * (1.0 / (n1 * n2))
    pitch = n2 + ROW_PAD
    na = n1 // TW_B
    slabs = min(FFT_SLABS, na)
    groups = na // slabs

    _stage1(n1, n1, n2, m1_ref, lambda c: k_ref[pl.ds(c, n1, stride=pitch), :], o_ref)

    def slab_stage(i, carry):
        b = i // groups
        a0 = (i % groups) * slabs
        rows = [_slab_rows(n2, na, a0 + q, b) for q in range(slabs)]
        xs = [_cmul(o_ref[pl.ds(rows[q], 2 * n2), :], ta_ref[a0 + q]) for q in range(slabs)]
        f = _mm(m2b_ref[b], _lane_cat(xs))
        for q in range(slabs):
            fr, fi = _split_ri(_lane_part(f, q) * scale)
            o_ref[pl.ds(rows[q], 2 * n2), :] = _join_ri(fr + skip, fi)
            o_ref[pl.ds(rows[q] + 2 * n2, SLAB_PAD), :] = jnp.zeros((SLAB_PAD, LANES), F32)
        return carry

    lax.fori_loop(0, TW_B * groups, slab_stage, 0)


def _filter_fft(kern, l1, hy_bias, fc):
    npad, dh = kern.shape
    n1, n2 = fc["n1"], fc["n2"]
    nslab = n1 * _slab_pitch(n2)
    consts = [fc["m1_real"], fc["m2b"], fc["ta"]]
    return pl.pallas_call(
        functools.partial(_filter_fft_kernel, n1, n2),
        grid=(dh // LANES,),
        in_specs=[pl.BlockSpec((npad, LANES), lambda c: (0, c), pipeline_mode=pl.Buffered(1)),
                  pl.BlockSpec((1, LANES), lambda c: (0, c)),
                  pl.BlockSpec((1, LANES), lambda c: (0, c))]
        + [_const_spec(a) for a in consts],
        out_specs=pl.BlockSpec((None, nslab, LANES), lambda c: (c, 0, 0)),
        out_shape=jax.ShapeDtypeStruct((dh // LANES, nslab, LANES), F32),
        compiler_params=_params("parallel"),
        name="filter_fft",
    )(kern, l1, hy_bias.reshape(1, dh), *consts)


def _filter_mlp_kernel(z_ref, w1_ref, b1_ref, w2_ref, b2_ref, w3_ref, b3_ref, fr_ref, h_ref):
    dot = lambda a, b: jnp.dot(a, b, precision=HIGHEST, preferred_element_type=F32)
    h = jnp.sin(fr_ref[0:1, :] * (dot(z_ref[...], w1_ref[...]) + b1_ref[...]))
    h = jnp.sin(fr_ref[1:2, :] * (dot(h, w2_ref[...]) + b2_ref[...]))
    h_ref[...] = jnp.sin(fr_ref[2:3, :] * (dot(h, w3_ref[...]) + b3_ref[...]))


def _filter_out_kernel(l, tr, n2, h_ref, w4f_ref, w4b_ref, dl_ref, o_ref, l1_ref):
    rc = pl.program_id(1)
    hcat = h_ref[...]
    hf = jnp.dot(hcat, w4f_ref[...], precision=HIGHEST, preferred_element_type=F32)
    hb = jnp.dot(hcat, w4b_ref[...], precision=HIGHEST, preferred_element_type=F32)
    row = lax.broadcasted_iota(jnp.int32, (tr, LANES), 0) + rc * tr
    step = 1.0 / (l - 1)
    t_f = row.astype(F32) * step
    t_b = (l - row).astype(F32) * step
    kf = hf * jnp.exp(-t_f * dl_ref[...])
    kb = jnp.where(row == 0, 0.0, hb * jnp.exp(-t_b * dl_ref[...]))
    for half, val in ((0, kf), (1, kb)):
        def store(r, n, v, half=half):
            o_ref[half, r:r + n, :] = v
        _store_pitched(store, val, n2)
    part = jnp.sum(jnp.abs(kf), axis=0, keepdims=True) + jnp.sum(jnp.abs(kb), axis=0, keepdims=True)

    @pl.when(rc == 0)
    def _():
        l1_ref[...] = part

    @pl.when(rc > 0)
    def _():
        l1_ref[...] += part


def _filter_feats(l, bands):
    f32 = np.float32
    t = np.linspace(0.0, 1.0, l, dtype=f32)[:, None]
    w = (f32(2.0 * math.pi) * np.arange(l, dtype=f32)[:, None] / f32(l)).astype(f32)
    f = np.linspace(1e-4, bands - 1, bands, dtype=f32)[None, :]
    z = np.concatenate([t, np.cos(f * w), -np.sin(f * w)], axis=-1).astype(f32)
    z_rev = np.concatenate([z[:1], z[1:][::-1]], axis=0)
    return jnp.asarray(np.concatenate([z, z_rev], axis=-1))


def _hyena_filter(l, n2, f_w1, f_b1, f_w2, f_b2, f_w3, f_b3, f_w4, f_freq):
    emb, hid = f_w1.shape
    dh = f_w4.shape[1] // 2
    tr = min(l, 1024)
    z2 = _filter_feats(l, (emb - 1) // 2)
    bd = lambda w: jnp.kron(jnp.eye(2, dtype=F32), w)
    two = lambda v: jnp.concatenate([v, v], axis=-1)
    margs = [z2, bd(f_w1), two(f_b1)[None], bd(f_w2), two(f_b2)[None], bd(f_w3), two(f_b3)[None], two(f_freq)]
    full1 = lambda a: pl.BlockSpec(a.shape, lambda r: tuple(0 for _ in a.shape))
    hcat = pl.pallas_call(
        _filter_mlp_kernel,
        grid=(l // tr,),
        in_specs=[pl.BlockSpec((tr, 2 * emb), lambda r: (r, 0))] + [full1(a) for a in margs[1:]],
        out_specs=pl.BlockSpec((tr, 2 * hid), lambda r: (r, 0)),
        out_shape=jax.ShapeDtypeStruct((l, 2 * hid), F32),
        compiler_params=_params("parallel"),
        name="filter_mlp",
    )(*margs)
    zeros = jnp.zeros((hid, dh), F32)
    w4f = jnp.concatenate([f_w4[:, :dh], zeros], axis=0)
    w4b = jnp.concatenate([zeros, f_w4[:, dh:]], axis=0)
    max_decay = math.log(FILTER_TARGET) / FAST_DECAY
    min_decay = math.log(FILTER_TARGET) / SLOW_DECAY
    deltas = jnp.abs(jnp.linspace(min_decay, max_decay, dh, dtype=F32))[None, :]
    kern, l1 = pl.pallas_call(
        functools.partial(_filter_out_kernel, l, tr, n2),
        grid=(dh // LANES, l // tr),
        in_specs=[
            pl.BlockSpec((tr, 2 * hid), lambda c, r: (r, 0)),
            pl.BlockSpec((2 * hid, LANES), lambda c, r: (0, c)),
            pl.BlockSpec((2 * hid, LANES), lambda c, r: (0, c)),
            pl.BlockSpec((1, LANES), lambda c, r: (0, c)),
        ],
        out_specs=[pl.BlockSpec((2, _pitched_rows(tr, n2), LANES), lambda c, r: (0, r, c)),
                   pl.BlockSpec((1, LANES), lambda c, r: (0, c))],
        out_shape=[jax.ShapeDtypeStruct((2, _pitched_rows(l, n2), dh), F32), jax.ShapeDtypeStruct((1, dh), F32)],
        compiler_params=_params("parallel", "arbitrary"),
        name="filter_out",
    )(hcat, w4f, w4b, deltas)
    return kern.reshape(2 * _pitched_rows(l, n2), dh), l1


def _grid_sincos(rows, dim):
    r, col = jnp.meshgrid(jnp.arange(rows, dtype=F32), jnp.arange(GRID_W, dtype=F32), indexing='ij')
    quarter = dim // 4
    omega = 1.0 / (10000.0 ** (jnp.arange(quarter, dtype=F32) / quarter))

    def emb(pos):
        ang = pos.reshape(-1, 1) * omega[None, :]
        return jnp.concatenate([jnp.sin(ang), jnp.cos(ang)], axis=-1)

    return jnp.concatenate([emb(r), emb(col)], axis=-1)


def kernel(x, c, ctx, c_ctx, w_mod, b_mod, norm_g, w_in, hy_conv_w, hy_conv_b, f_w1, f_b1, f_w2, f_b2,
           f_w3, f_b3, f_w4, f_freq, hy_bias, rg_conv_w, rg_conv_b, rg_wa, rg_ba, rg_wx, rg_bx, rg_lam,
           br_norm_h, br_norm_r, w_out, final_g):
    B, N, D = x.shape
    L_ctx = ctx.shape[1]
    depth = w_mod.shape[0]
    dh = hy_bias.shape[1]
    dr = rg_conv_b.shape[1]
    assert B == NB and B % BB == 0 and dh == dr

    pos = _grid_sincos(N // GRID_W, D)
    cond = jnp.concatenate([c, jnp.broadcast_to(c_ctx[None], (8, D))], axis=0)
    mod = _modulation(cond, w_mod, b_mod)
    w_in_bf = w_in.astype(BF16)
    w_out_bf = w_out.astype(BF16)
    fc_lat = _fft_consts(N)
    fc_ctx = _fft_consts(L_ctx)

    n2_l, n2_c = fc_lat["n2"], fc_ctx["n2"]
    state = lambda h, rows: h[:, rows, :].transpose(1, 0, 2).reshape(NB, dr)

    xl, xc = x, ctx
    for l in range(depth):
        last = l == depth - 1
        ml = mod[l, :B].reshape(B, 1, 3 * D)
        mc = jnp.broadcast_to(mod[l, B:B + 1].reshape(1, 1, 3 * D), (B, 1, 3 * D))
        sh_l, sc_l, g_l = ml[..., :D], ml[..., D:2 * D], ml[..., 2 * D:]
        sh_c, sc_c, g_c = mc[..., :D], mc[..., D:2 * D], mc[..., 2 * D:]
        proj = (norm_g[l], w_in_bf[l], hy_conv_w[l], hy_conv_b[l], dh, dr)
        if l == 0:
            u_l, x0_l, zh_l, zr_l, xr_l, xl = _inproj(xl, pos, sh_l, sc_l, *proj, n2_l)
        else:
            u_l, x0_l, zh_l, zr_l, xr_l = _inproj(xl, None, sh_l, sc_l, *proj, n2_l)
        u_c, x0_c, zh_c, zr_c, xr_c = _inproj(xc, None, sh_c, sc_c, *proj, n2_c)
        rnn_w = _rglru_weights(rg_wa[l], rg_ba[l], rg_wx[l], rg_bx[l])
        zero = jnp.zeros((NB, dr), F32)
        hf_c, hb_c = _rglru(xr_c, rg_conv_w[l], rg_conv_b[l], *rnn_w, rg_lam[l], zero, zero)
        hf_l, hb_l = _rglru(xr_l, rg_conv_w[l], rg_conv_b[l], *rnn_w, rg_lam[l],
                            state(hf_c, slice(-NB, None)), state(hb_c, slice(0, NB)))
        filt = (f_w1[l], f_b1[l], f_w2[l], f_b2[l], f_w3[l], f_b3[l], f_w4[l], f_freq[l])
        shared = (br_norm_h[l], br_norm_r[l], w_out_bf[l])
        cv_l = _fftconv(u_l, _filter_fft(*_hyena_filter(N, n2_l, *filt), hy_bias[l], fc_lat), fc_lat)
        if not last:
            cv_c = _fftconv(u_c, _filter_fft(*_hyena_filter(L_ctx, n2_c, *filt), hy_bias[l], fc_ctx), fc_ctx)
            xc = _outproj(cv_c, x0_c, n2_c, hf_c, hb_c, zh_c, zr_c, *shared, xc, g_c, None)
        xl = _outproj(cv_l, x0_l, n2_l, hf_l, hb_l, zh_l, zr_l, *shared, xl, g_l,
                      final_g if last else None)
    return xl
```

```python
import functools
import math

import jax
import jax.numpy as jnp
import numpy as np
from jax import lax
from jax.experimental import pallas as pl
from jax.experimental.pallas import tpu as pltpu

GRID_W = 64
FILTER_TARGET = 1e-2
FAST_DECAY = 0.3
SLOW_DECAY = 1.5
RG_C = 8.0
EPS = 1e-6

LANES = 128
F32 = jnp.float32
BF16 = jnp.bfloat16
HIGHEST = lax.Precision.HIGHEST

VMEM_LIMIT_BYTES = 56 * 1024 * 1024


def _params(*sem):
    return pltpu.CompilerParams(dimension_semantics=sem, vmem_limit_bytes=VMEM_LIMIT_BYTES)


def _silu(x):
    return x * jax.nn.sigmoid(x)


def _rms(x, g):
    return x * lax.rsqrt(jnp.mean(x * x, axis=-1, keepdims=True) + EPS) * g


def _mod_kernel(s_ref, w_ref, b_ref, o_ref):
    s = _silu(s_ref[...])
    o_ref[0] = jnp.dot(s, w_ref[0], precision=HIGHEST, preferred_element_type=F32) + b_ref[0]


def _modulation(cond, w_mod, b_mod):
    depth, d, d3 = w_mod.shape
    rows = cond.shape[0]
    tn = d // 2
    return pl.pallas_call(
        _mod_kernel,
        grid=(depth, d3 // tn),
        in_specs=[
            pl.BlockSpec((rows, d), lambda l, j: (0, 0)),
            pl.BlockSpec((1, d, tn), lambda l, j: (l, 0, j)),
            pl.BlockSpec((1, 1, tn), lambda l, j: (l, 0, j)),
        ],
        out_specs=pl.BlockSpec((1, rows, tn), lambda l, j: (l, 0, j)),
        out_shape=jax.ShapeDtypeStruct((depth, rows, d3), F32),
        compiler_params=_params("parallel", "parallel"),
        name="modulation",
    )(cond, w_mod, b_mod.reshape(depth, 1, d3))


NB = 8
HALO = 8
BB = 4
GROUP = 2


def _inproj_kernel(add_pos, tm, dh, dr, n2, *refs):
    refs = list(refs)
    x_ref, xp_ref, xn_ref = refs[:3]
    refs = refs[3:]
    if add_pos:
        pos_ref, posp_ref, posn_ref = refs[:3]
        refs = refs[3:]
    (sh_ref, sc_ref, g_ref, wh_ref, wr_ref, cw_ref, cb_ref, u_ref, x0_ref, zh_ref, zr_ref, xr_ref) = refs[:12]
    j = pl.program_id(0)
    i = pl.program_id(1)
    keep_prev = (j > 0).astype(F32)
    keep_next = (j < pl.num_programs(0) - 1).astype(F32)
    seg = tm + 2 * HALO

    for b0 in range(0, BB, GROUP):
        xs = []
        for b in range(b0, b0 + GROUP):
            xb = jnp.concatenate([xp_ref[b], x_ref[b], xn_ref[b]], axis=0)
            if add_pos:
                xb = xb + jnp.concatenate([posp_ref[...], pos_ref[...], posn_ref[...]], axis=0)
                refs[12][b] = xb[HALO:HALO + tm]
            xs.append((_rms(xb, g_ref[...]) * (1.0 + sc_ref[b]) + sh_ref[b]).astype(BF16))

        ph = jnp.dot(jnp.concatenate(xs, axis=0), wh_ref[...], preferred_element_type=F32)
        for k, b in enumerate(range(b0, b0 + GROUP)):
            pb = ph[k * seg:(k + 1) * seg]
            full = jnp.concatenate([pb[:HALO] * keep_prev, pb[HALO:HALO + tm], pb[HALO + tm:] * keep_next], axis=0)
            c = cb_ref[...] + cw_ref[0:1, :] * full[HALO - 1:HALO - 1 + tm]
            c = c + cw_ref[1:2, :] * full[HALO:HALO + tm]
            c = c + cw_ref[2:3, :] * full[HALO + 1:HALO + 1 + tm]
            x0_ref[b] = c[:, :dh].astype(x0_ref.dtype)

            def store(r, n, v, b=b):
                for s_ in range(dh // LANES):
                    u_ref[s_, b, r:r + n, :] = _lane_part(v, s_)
            _store_pitched(store, c[:, 2 * dh:] * c[:, dh:2 * dh], n2)

        xn_mid = jnp.concatenate([x[HALO:HALO + tm] for x in xs], axis=0)
        pr = jnp.dot(xn_mid, wr_ref[...], preferred_element_type=F32)
        for k, b in enumerate(range(b0, b0 + GROUP)):
            pb = pr[k * tm:(k + 1) * tm]
            zh_ref[b] = pb[:, :dh].astype(zh_ref.dtype)
            zr_ref[b] = pb[:, dh + dr:].astype(zr_ref.dtype)
            for s_ in range(dr // LANES):
                xr_ref[s_, pl.ds(i * BB + b, tm, stride=NB), :] = pb[:, dh + s_ * LANES:dh + (s_ + 1) * LANES]


def _inproj(x, pos, sh, sc, g, w_bf16, conv_w, conv_b, dh, dr, n2):
    b, l, d = x.shape
    tm = min(l, 128)
    add_pos = pos is not None
    nt = l // HALO
    per = tm // HALO
    row = lambda w: pl.BlockSpec((BB, tm, w), lambda j, i: (i, j, 0))
    prev = lambda j: jnp.maximum(j * per - 1, 0)
    nxt = lambda j: jnp.minimum((j + 1) * per, nt - 1)
    vec = pl.BlockSpec((BB, 1, d), lambda j, i: (i, 0, 0))
    full = lambda a: pl.BlockSpec(a.shape, lambda j, i: tuple(0 for _ in a.shape), pipeline_mode=pl.Buffered(1))
    in_specs = [row(d),
                pl.BlockSpec((BB, HALO, d), lambda j, i: (i, prev(j), 0)),
                pl.BlockSpec((BB, HALO, d), lambda j, i: (i, nxt(j), 0))]
    args = [x, x, x]
    if add_pos:
        in_specs += [pl.BlockSpec((tm, d), lambda j, i: (j, 0)),
                     pl.BlockSpec((HALO, d), lambda j, i: (prev(j), 0)),
                     pl.BlockSpec((HALO, d), lambda j, i: (nxt(j), 0))]
        args += [pos, pos, pos]
    w_h = w_bf16[:, :3 * dh]
    w_r = w_bf16[:, 3 * dh:]
    consts = [g.reshape(1, d), w_h, w_r, conv_w, conv_b.reshape(1, 3 * dh)]
    in_specs += [vec, vec] + [full(a) for a in consts]
    args += [sh, sc] + consts
    out_specs = [pl.BlockSpec((dh // LANES, BB, _pitched_rows(tm, n2), LANES), lambda j, i: (0, i, j, 0)),
                 row(dh), row(dh), row(dr),
                 pl.BlockSpec((dr // LANES, tm * NB, LANES), lambda j, i: (0, j, 0))]
    out_shape = [jax.ShapeDtypeStruct((dh // LANES, b, _pitched_rows(l, n2), LANES), F32),
                 jax.ShapeDtypeStruct((b, l, dh), BF16), jax.ShapeDtypeStruct((b, l, dh), BF16),
                 jax.ShapeDtypeStruct((b, l, dr), BF16),
                 jax.ShapeDtypeStruct((dr // LANES, l * NB, LANES), F32)]
    if add_pos:
        out_specs.append(row(d))
        out_shape.append(jax.ShapeDtypeStruct((b, l, d), F32))
    return pl.pallas_call(
        functools.partial(_inproj_kernel, add_pos, tm, dh, dr, n2),
        grid=(l // tm, b // BB),
        in_specs=in_specs,
        out_specs=out_specs,
        out_shape=out_shape,
        compiler_params=_params("arbitrary", "arbitrary"),
        name="inproj_pos" if add_pos else "inproj",
    )(*args)


def _outproj_kernel(final, tm, n2, *refs):
    if final:
        (cv_ref, x0_ref, hf_ref, hb_ref, zh_ref, zr_ref, gnh_ref, gnr_ref, wt_ref, wb_ref,
         x_ref, gate_ref, fg_ref, o_ref) = refs
    else:
        (cv_ref, x0_ref, hf_ref, hb_ref, zh_ref, zr_ref, gnh_ref, gnr_ref, wt_ref, wb_ref,
         x_ref, gate_ref, o_ref) = refs
    i = pl.program_id(1)
    yhs, yrs = [], []
    for b in range(BB):
        cv = _lane_cat([_load_pitched(lambda r, n, b=b, s_=s_: cv_ref[s_, b, r:r + n, :], tm, n2)
                        for s_ in range(cv_ref.shape[0])])
        y_h = cv * x0_ref[b].astype(F32)
        yhs.append((_rms(y_h, gnh_ref[...]) * _silu(zh_ref[b].astype(F32))).astype(BF16))
        hs = _lane_cat([hf_ref[s_, pl.ds(i * BB + b, tm, stride=NB), :] + hb_ref[s_, pl.ds(i * BB + b, tm, stride=NB), :]
                        for s_ in range(hf_ref.shape[0])])
        yrs.append((_rms(hs, gnr_ref[...]) * _silu(zr_ref[b].astype(F32))).astype(BF16))
    y = jnp.dot(jnp.concatenate(yhs, axis=0), wt_ref[...], preferred_element_type=F32)
    y = y + jnp.dot(jnp.concatenate(yrs, axis=0), wb_ref[...], preferred_element_type=F32)
    for b in range(BB):
        out = x_ref[b] + gate_ref[b] * y[b * tm:(b + 1) * tm]
        if final:
            out = _rms(out, fg_ref[...])
        o_ref[b] = out


def _outproj(conv, x0, n2, hf, hb, zh, zr, gn_h, gn_r, w_out_bf16, x, gate, final_g):
    b, l, d = x.shape
    dh = x0.shape[-1]
    dr = zr.shape[-1]
    tm = min(l, 128)
    final = final_g is not None
    row = lambda w: pl.BlockSpec((BB, tm, w), lambda j, i: (i, j, 0))
    pitched = pl.BlockSpec((dh // LANES, BB, _pitched_rows(tm, n2), LANES), lambda j, i: (0, i, j, 0))
    slabs = pl.BlockSpec((dr // LANES, tm * NB, LANES), lambda j, i: (0, j, 0))
    full = lambda a: pl.BlockSpec(a.shape, lambda j, i: tuple(0 for _ in a.shape), pipeline_mode=pl.Buffered(1))
    consts = [gn_h.reshape(1, dh), gn_r.reshape(1, dr), w_out_bf16[:dh], w_out_bf16[dh:]]
    in_specs = [pitched, row(dh), slabs, slabs, row(dh), row(dr)] + [full(a) for a in consts] + [
        row(d), pl.BlockSpec((BB, 1, d), lambda j, i: (i, 0, 0))]
    args = [conv, x0, hf, hb, zh, zr] + consts + [x, gate]
    if final:
        fg = final_g.reshape(1, d)
        in_specs.append(full(fg))
        args.append(fg)
    return pl.pallas_call(
        functools.partial(_outproj_kernel, final, tm, n2),
        grid=(l // tm, b // BB),
        in_specs=in_specs,
        out_specs=row(d),
        out_shape=jax.ShapeDtypeStruct((b, l, d), F32),
        compiler_params=_params("parallel", "parallel"),
        name="outproj_final" if final else "outproj",
    )(*args)


RG_PREV = 2
RG_NEXT = 1
SQRT_FLOOR = 1e-30
LOG2E = math.log2(math.e)


def _softplus(x):
    return jnp.maximum(x, 0.0) + jnp.log(1.0 + jnp.exp(-jnp.abs(x)))


def _sqrt01(y):
    return y * lax.rsqrt(jnp.maximum(y, SQRT_FLOOR))


def _rglru_kernel(tc, xf_ref, xfp_ref, xfn_ref, xb_ref, xbp_ref, xbn_ref, cw_ref, cb_ref,
                  wf_ref, bf_ref, wb_ref, bb_ref, lam_ref, h0f_ref, h0b_ref,
                  hf_ref, hb_ref, af_s, bf_s, ab_s, bb_s, cf_s, cbk_s):
    j = pl.program_id(0)
    nj = pl.num_programs(0)
    rows = tc * NB
    ns = xf_ref.shape[0]
    dr = ns * LANES
    wide = lambda ref: _lane_cat([ref[s_] for s_ in range(ns)])

    @pl.when(j == 0)
    def _():
        cf_s[...] = h0f_ref[...]
        cbk_s[...] = h0b_ref[...]

    def coeffs(x_ref, xp_ref, xn_ref, first, last, w_ref, b_ref, lam, a_s, b_s):
        prev = jnp.where(first, 0.0, wide(xp_ref))
        nxt = jnp.where(last, 0.0, wide(xn_ref))
        xfull = jnp.concatenate([prev, wide(x_ref), nxt], axis=0)
        xh = cb_ref[...] + cw_ref[0:1, :] * xfull[0:rows]
        for k in range(1, RG_PREV + RG_NEXT + 1):
            xh = xh + cw_ref[k:k + 1, :] * xfull[k * NB:k * NB + rows]
        g = jnp.dot(xh.astype(BF16), w_ref[...], preferred_element_type=F32) + b_ref[...]
        t_r = jnp.tanh(g[:, :dr])
        t_i = jnp.tanh(g[:, dr:])
        c1 = (-0.5 * RG_C * LOG2E) * _softplus(-lam)
        a = jnp.exp2(c1 * t_r + c1)
        a_s[...] = a
        b_s[...] = _sqrt01(1.0 - a * a) * (t_i * xh + xh)

    coeffs(xf_ref, xfp_ref, xfn_ref, j == 0, j == nj - 1, wf_ref, bf_ref, lam_ref[0:1, :], af_s, bf_s)
    coeffs(xb_ref, xbp_ref, xbn_ref, j == nj - 1, j == 0, wb_ref, bb_ref, lam_ref[1:2, :], ab_s, bb_s)

    def step(t, carry):
        hf, hb = carry
        rf = pl.multiple_of(t * NB, NB)
        rb = pl.multiple_of((tc - 1 - t) * NB, NB)
        hf = af_s[pl.ds(rf, NB), :] * hf + bf_s[pl.ds(rf, NB), :]
        hb = ab_s[pl.ds(rb, NB), :] * hb + bb_s[pl.ds(rb, NB), :]
        for s_ in range(ns):
            hf_ref[s_, pl.ds(rf, NB), :] = _lane_part(hf, s_)
            hb_ref[s_, pl.ds(rb, NB), :] = _lane_part(hb, s_)
        return hf, hb

    hf, hb = lax.fori_loop(0, tc, step, (cf_s[...], cbk_s[...]), unroll=8)
    cf_s[...] = hf
    cbk_s[...] = hb


def _rglru(xr2, conv_w, conv_b, w_f, b_f, w_b, b_b, lam, h0_f, h0_b):
    ns, rows_total, _ = xr2.shape
    dr = ns * LANES
    l = rows_total // NB
    tc = min(l, 128)
    nj = l // tc
    rows = tc * NB
    prev_rows = RG_PREV * NB
    last_prev = rows // prev_rows
    n_next = rows_total // NB

    def blk(shape, fn):
        return pl.BlockSpec(shape, fn)

    full = lambda shape: pl.BlockSpec(shape, lambda j: tuple(0 for _ in shape))
    in_specs = [
        blk((ns, rows, LANES), lambda j: (0, j, 0)),
        blk((ns, prev_rows, LANES), lambda j: (0, jnp.maximum(j * last_prev - 1, 0), 0)),
        blk((ns, NB, LANES), lambda j: (0, jnp.minimum((j + 1) * tc, n_next - 1), 0)),
        blk((ns, rows, LANES), lambda j: (0, nj - 1 - j, 0)),
        blk((ns, prev_rows, LANES), lambda j: (0, jnp.maximum((nj - 1 - j) * last_prev - 1, 0), 0)),
        blk((ns, NB, LANES), lambda j: (0, jnp.minimum((nj - j) * tc, n_next - 1), 0)),
        full(conv_w.shape), full((1, dr)),
        full(w_f.shape), full((1, 2 * dr)), full(w_b.shape), full((1, 2 * dr)),
        full(lam.shape), full((NB, dr)), full((NB, dr)),
    ]
    out_specs = [blk((ns, rows, LANES), lambda j: (0, j, 0)), blk((ns, rows, LANES), lambda j: (0, nj - 1 - j, 0))]
    out_shape = [jax.ShapeDtypeStruct((ns, rows_total, LANES), F32)] * 2
    scratch = [pltpu.VMEM((rows, dr), F32)] * 4 + [pltpu.VMEM((NB, dr), F32)] * 2
    return pl.pallas_call(
        functools.partial(_rglru_kernel, tc),
        grid=(nj,),
        in_specs=in_specs,
        out_specs=out_specs,
        out_shape=out_shape,
        scratch_shapes=scratch,
        compiler_params=_params("arbitrary"),
        name="rglru",
    )(xr2, xr2, xr2, xr2, xr2, xr2, 0.5 * conv_w, 0.5 * conv_b.reshape(1, dr), w_f, 0.5 * b_f, w_b, 0.5 * b_b,
      lam, h0_f, h0_b)


def _block_diag(w):
    heads, blk, _ = w.shape
    eye = jnp.eye(heads, dtype=w.dtype)
    return jnp.einsum('hij,hg->higj', w, eye).reshape(heads * blk, heads * blk)


def _rglru_weights(wa, ba, wx, bx):
    out = []
    for d in range(2):
        w = jnp.concatenate([_block_diag(wa[d]), _block_diag(wx[d])], axis=1).astype(BF16)
        b = jnp.concatenate([ba[d], bx[d]])[None, :]
        out += [w, b]
    return out


TW_B = 8
ROW_PAD = 8
SLAB_PAD = 4
MIN_STREAM_BYTES = 1 << 20
FFT_COLS = 32
FFT_SLABS = 16


def _fft_dims(l):
    n = 2 * l
    n1 = 16
    while (2 * n1) * (2 * n1) <= n:
        n1 *= 2
    return n1, n // n1


def _pitched_rows(rows, n2):
    return rows // n2 * (n2 + ROW_PAD)


def _store_pitched(store, val, n2):
    p = n2 + ROW_PAD
    pad = jnp.zeros((ROW_PAD, val.shape[1]), val.dtype)
    for g in range(val.shape[0] // n2):
        store(g * p, n2, val[g * n2:(g + 1) * n2])
        store(g * p + n2, ROW_PAD, pad)


def _load_pitched(load, rows, n2):
    p = n2 + ROW_PAD
    return jnp.concatenate([load(g * p, n2) for g in range(rows // n2)], axis=0)


def _slab_pitch(n2):
    return 2 * n2 + SLAB_PAD


def _fft_consts(l):
    n1, n2 = _fft_dims(l)
    n = n1 * n2
    h = n1 // 2
    f1 = np.exp(-2j * np.pi * np.outer(np.arange(n1), np.arange(n1)) / n1)
    f2 = np.exp(-2j * np.pi * np.outer(np.arange(n2), np.arange(n2)) / n2)
    stack = lambda a: np.block([[a.real, -a.imag], [a.imag, a.real]])
    bf = lambda a: jnp.asarray(a, F32).astype(BF16)

    g, r = np.divmod(np.arange(2 * n2), 16)
    perm = np.where(r < 8, 8 * g + r, n2 + 8 * g + (r - 8))
    inter = lambda a: a[perm][:, perm]

    m1_full = stack(f1)
    m1_half = np.concatenate([m1_full[:, :h], m1_full[:, n1:n1 + h]], axis=1)
    m3_full = stack(np.conj(f1))
    m3 = np.concatenate([m3_full[:h], m3_full[n1:n1 + h]], axis=0)
    na = n1 // TW_B
    p = np.arange(n1)
    k1_of_p = TW_B * (p % na) + p // na
    order = np.concatenate([k1_of_p, n1 + k1_of_p])
    m1_half, m1_real, m3 = m1_half[order], m1_full[:, :n1][order], m3[:, order]
    ang_a = -2.0 * np.pi * np.outer(np.arange(na) * TW_B, np.arange(n2)) / n
    ang_b = -2.0 * np.pi * np.outer(np.arange(TW_B), np.arange(n2)) / n
    ta = np.concatenate([np.cos(ang_a), np.sin(ang_a)], axis=1)[:, perm]
    ta = jnp.asarray(np.broadcast_to(ta[..., None], ta.shape + (LANES,)), F32)
    tw_b = np.exp(1j * ang_b)
    fwd = [inter(stack(f2 * tw_b[b][None, :])) for b in range(TW_B)]
    inv = [inter(stack(np.conj(f2) * np.conj(tw_b[b])[:, None])) for b in range(TW_B)]
    return dict(n1=n1, n2=n2, m1=bf(m1_half), m3=bf(m3), ta=ta,
                m2b=jnp.stack([bf(m) for m in fwd]), m2ib=jnp.stack([bf(m) for m in inv]),
                m1_real=bf(m1_real))


def _mm(m, x):
    return jnp.dot(m, x.astype(BF16), preferred_element_type=F32)


def _lane_cat(parts):
    return parts[0] if len(parts) == 1 else jnp.concatenate(parts, axis=1)


def _lane_part(x, i):
    return x[:, i * LANES:(i + 1) * LANES]


def _split_ri(x):
    x4 = x.reshape(x.shape[0] // 16, 2, 8, x.shape[1])
    return x4[:, 0], x4[:, 1]


def _join_ri(re, im):
    return jnp.stack([re, im], axis=1).reshape(2 * re.shape[0] * 8, re.shape[2])


def _cmul(x, t, conj=False):
    xr, xi = _split_ri(x)
    tr, ti = _split_ri(t)
    if conj:
        return _join_ri(xr * tr + xi * ti, xi * tr - xr * ti)
    return _join_ri(xr * tr - xi * ti, xr * ti + xi * tr)


def _col_rows(c):
    return (c // 8) * 16 + c % 8


def _stage1(n_in, n1, n2, m_ref, load, a_ref):
    cols = min(FFT_COLS, n2)
    sp = _slab_pitch(n2)

    def body(i, carry):
        out = _mm(m_ref[...], _lane_cat([load(i * cols + q) for q in range(cols)]))
        for q in range(cols):
            r = _col_rows(i * cols + q)
            a_ref[pl.ds(r, n1, stride=sp), :] = _lane_part(out[:n1], q)
            a_ref[pl.ds(r + 8, n1, stride=sp), :] = _lane_part(out[n1:], q)
        return carry

    lax.fori_loop(0, n2 // cols, body, 0)


def _slab_rows(n2, na, a, b):
    return pl.multiple_of((b * na + a) * _slab_pitch(n2), SLAB_PAD)


def _fftconv_kernel(n1, n2, u_hbm, ks_hbm, m1_ref, m2b_ref, m2ib_ref, m3_ref, ta_ref, y_hbm,
                    a_ref, buf, kbuf, sem_in, sem_out, sem_k):
    h = n1 // 2
    pitch = n2 + ROW_PAD
    sp = _slab_pitch(n2)
    na = n1 // TW_B
    slabs = min(FFT_SLABS, na)
    groups = na // slabs
    cols = min(FFT_COLS, n2)
    npairs = pl.num_programs(1)
    total = pl.num_programs(0) * npairs
    t = pl.program_id(0) * npairs + pl.program_id(1)
    slot = t % 2

    def window(ref, tt):
        return ref.at[tt // npairs, pl.ds(2 * (tt % npairs), 2)]

    def in_copy(tt, s):
        return pltpu.make_async_copy(window(u_hbm, tt), buf.at[s], sem_in.at[s])

    def out_copy(tt, s):
        return pltpu.make_async_copy(buf.at[s], window(y_hbm, tt), sem_out.at[s])

    nchunks = TW_B * groups if slabs * sp * LANES * 4 >= MIN_STREAM_BYTES else 1
    chunk = (n1 * sp) // nchunks
    first_pair = pl.program_id(1) == 0

    def k_copy(i):
        rows0 = pl.multiple_of(i * chunk, 8)
        return pltpu.make_async_copy(ks_hbm.at[pl.program_id(0), pl.ds(rows0, chunk)],
                                     kbuf.at[pl.ds(rows0, chunk)], sem_k.at[i % 2])

    @pl.when(first_pair)
    def _():
        k_copy(0).start()

    @pl.when(t == 0)
    def _():
        in_copy(t, slot).start()

    in_copy(t, slot).wait()
    u_ref = buf.at[slot]

    def load_u(c):
        return jnp.concatenate([u_ref[0, pl.ds(c, h, stride=pitch), :],
                                u_ref[1, pl.ds(c, h, stride=pitch), :]], axis=0)

    _stage1(h, n1, n2, m1_ref, load_u, a_ref)

    @pl.when(t >= 1)
    def _():
        out_copy(t - 1, 1 - slot).wait()

    @pl.when(t + 1 < total)
    def _():
        in_copy(t + 1, 1 - slot).start()

    if nchunks == 1:
        @pl.when(first_pair)
        def _():
            k_copy(0).wait()

    def slab_stage(i, carry):
        b = i // groups
        a0 = (i % groups) * slabs
        if nchunks > 1:
            @pl.when(first_pair)
            def _():
                k_copy(i).wait()

            @pl.when(first_pair & (i + 1 < nchunks))
            def _():
                k_copy(i + 1).start()

        rows = [_slab_rows(n2, na, a0 + q, b) for q in range(slabs)]
        xs = [_cmul(a_ref[pl.ds(rows[q], 2 * n2), :], ta_ref[a0 + q]) for q in range(slabs)]
        f = _mm(m2b_ref[b], _lane_cat(xs))
        cs = [_cmul(_lane_part(f, q), kbuf[pl.ds(rows[q], 2 * n2), :]) for q in range(slabs)]
        d = _mm(m2ib_ref[b], _lane_cat(cs))
        for q in range(slabs):
            a_ref[pl.ds(rows[q], 2 * n2), :] = _cmul(_lane_part(d, q), ta_ref[a0 + q], conj=True)
        return carry

    lax.fori_loop(0, TW_B * groups, slab_stage, 0)

    def stage3(i, carry):
        xs = []
        for q in range(cols):
            r = _col_rows(i * cols + q)
            xs.append(jnp.concatenate([a_ref[pl.ds(r, n1, stride=sp), :],
                                       a_ref[pl.ds(r + 8, n1, stride=sp), :]], axis=0))
        out = _mm(m3_ref[...], _lane_cat(xs))
        for q in range(cols):
            c = i * cols + q
            u_ref[0, pl.ds(c, h, stride=pitch), :] = _lane_part(out[:h], q)
            u_ref[1, pl.ds(c, h, stride=pitch), :] = _lane_part(out[h:], q)
        return carry

    lax.fori_loop(0, n2 // cols, stage3, 0)

    out_copy(t, slot).start()

    @pl.when(t == total - 1)
    def _():
        out_copy(t, slot).wait()


def _const_spec(a):
    return pl.BlockSpec(a.shape, lambda *_: tuple(0 for _ in a.shape), pipeline_mode=pl.Buffered(1))


def _fftconv(u, kspec, fc):
    ns, b, lp, _ = u.shape
    dh = ns * LANES
    n1, n2 = fc["n1"], fc["n2"]
    nslab = n1 * _slab_pitch(n2)
    consts = [fc["m1"], fc["m2b"], fc["m2ib"], fc["m3"], fc["ta"]]
    return pl.pallas_call(
        functools.partial(_fftconv_kernel, n1, n2),
        grid=(dh // LANES, b // 2),
        in_specs=[
            pl.BlockSpec(memory_space=pl.ANY),
            pl.BlockSpec(memory_space=pl.ANY),
        ] + [_const_spec(a) for a in consts],
        out_specs=pl.BlockSpec(memory_space=pl.ANY),
        out_shape=jax.ShapeDtypeStruct(u.shape, F32),
        scratch_shapes=[pltpu.VMEM((nslab, LANES), F32), pltpu.VMEM((2, 2, lp, LANES), F32),
                        pltpu.VMEM((nslab, LANES), F32),
                        pltpu.SemaphoreType.DMA((2,)), pltpu.SemaphoreType.DMA((2,)), pltpu.SemaphoreType.DMA((2,))],
        compiler_params=_params("arbitrary", "arbitrary"),
        name="fftconv",
    )(u, kspec, *consts)


def _filter_fft_kernel(n1, n2, k_ref, l1_ref, hyb_ref, m1_ref, m2b_ref, ta_ref, o_ref):
    scale = (1.0 / (n1 * n2)) / l1_ref[...]
    skip = hyb_ref[...] * (1.0 / (n1 * n2))
    pitch = n2 + ROW_PAD
    na = n1 // TW_B
    slabs = min(FFT_SLABS, na)
    groups = na // slabs

    _stage1(n1, n1, n2, m1_ref, lambda c: k_ref[pl.ds(c, n1, stride=pitch), :], o_ref)

    def slab_stage(i, carry):
        b = i // groups
        a0 = (i % groups) * slabs
        rows = [_slab_rows(n2, na, a0 + q, b) for q in range(slabs)]
        xs = [_cmul(o_ref[pl.ds(rows[q], 2 * n2), :], ta_ref[a0 + q]) for q in range(slabs)]
        f = _mm(m2b_ref[b], _lane_cat(xs))
        for q in range(slabs):
            fr, fi = _split_ri(_lane_part(f, q) * scale)
            o_ref[pl.ds(rows[q], 2 * n2), :] = _join_ri(fr + skip, fi)
            o_ref[pl.ds(rows[q] + 2 * n2, SLAB_PAD), :] = jnp.zeros((SLAB_PAD, LANES), F32)
        return carry

    lax.fori_loop(0, TW_B * groups, slab_stage, 0)


def _filter_fft(kern, l1, hy_bias, fc):
    npad, dh = kern.shape
    n1, n2 = fc["n1"], fc["n2"]
    nslab = n1 * _slab_pitch(n2)
    consts = [fc["m1_real"], fc["m2b"], fc["ta"]]
    return pl.pallas_call(
        functools.partial(_filter_fft_kernel, n1, n2),
        grid=(dh // LANES,),
        in_specs=[pl.BlockSpec((npad, LANES), lambda c: (0, c), pipeline_mode=pl.Buffered(1)),
                  pl.BlockSpec((1, LANES), lambda c: (0, c)),
                  pl.BlockSpec((1, LANES), lambda c: (0, c))]
        + [_const_spec(a) for a in consts],
        out_specs=pl.BlockSpec((None, nslab, LANES), lambda c: (c, 0, 0)),
        out_shape=jax.ShapeDtypeStruct((dh // LANES, nslab, LANES), F32),
        compiler_params=_params("parallel"),
        name="filter_fft",
    )(kern, l1, hy_bias.reshape(1, dh), *consts)


def _filter_mlp_kernel(z_ref, w1_ref, b1_ref, w2_ref, b2_ref, w3_ref, b3_ref, fr_ref, h_ref):
    dot = lambda a, b: jnp.dot(a, b, precision=HIGHEST, preferred_element_type=F32)
    h = jnp.sin(fr_ref[0:1, :] * (dot(z_ref[...], w1_ref[...]) + b1_ref[...]))
    h = jnp.sin(fr_ref[1:2, :] * (dot(h, w2_ref[...]) + b2_ref[...]))
    h_ref[...] = jnp.sin(fr_ref[2:3, :] * (dot(h, w3_ref[...]) + b3_ref[...]))


def _filter_out_kernel(l, tr, n2, h_ref, w4f_ref, w4b_ref, dl_ref, o_ref, l1_ref):
    rc = pl.program_id(1)
    hcat = h_ref[...]
    hf = jnp.dot(hcat, w4f_ref[...], precision=HIGHEST, preferred_element_type=F32)
    hb = jnp.dot(hcat, w4b_ref[...], precision=HIGHEST, preferred_element_type=F32)
    row = lax.broadcasted_iota(jnp.int32, (tr, LANES), 0) + rc * tr
    step = 1.0 / (l - 1)
    t_f = row.astype(F32) * step
    t_b = (l - row).astype(F32) * step
    kf = hf * jnp.exp(-t_f * dl_ref[...])
    kb = jnp.where(row == 0, 0.0, hb * jnp.exp(-t_b * dl_ref[...]))
    for half, val in ((0, kf), (1, kb)):
        def store(r, n, v, half=half):
            o_ref[half, r:r + n, :] = v
        _store_pitched(store, val, n2)
    part = jnp.sum(jnp.abs(kf), axis=0, keepdims=True) + jnp.sum(jnp.abs(kb), axis=0, keepdims=True)

    @pl.when(rc == 0)
    def _():
        l1_ref[...] = part

    @pl.when(rc > 0)
    def _():
        l1_ref[...] += part


def _filter_feats(l, bands):
    f32 = np.float32
    t = np.linspace(0.0, 1.0, l, dtype=f32)[:, None]
    w = (f32(2.0 * math.pi) * np.arange(l, dtype=f32)[:, None] / f32(l)).astype(f32)
    f = np.linspace(1e-4, bands - 1, bands, dtype=f32)[None, :]
    z = np.concatenate([t, np.cos(f * w), -np.sin(f * w)], axis=-1).astype(f32)
    z_rev = np.concatenate([z[:1], z[1:][::-1]], axis=0)
    return jnp.asarray(np.concatenate([z, z_rev], axis=-1))


def _hyena_filter(l, n2, f_w1, f_b1, f_w2, f_b2, f_w3, f_b3, f_w4, f_freq):
    emb, hid = f_w1.shape
    dh = f_w4.shape[1] // 2
    tr = min(l, 1024)
    z2 = _filter_feats(l, (emb - 1) // 2)
    bd = lambda w: jnp.kron(jnp.eye(2, dtype=F32), w)
    two = lambda v: jnp.concatenate([v, v], axis=-1)
    margs = [z2, bd(f_w1), two(f_b1)[None], bd(f_w2), two(f_b2)[None], bd(f_w3), two(f_b3)[None], two(f_freq)]
    full1 = lambda a: pl.BlockSpec(a.shape, lambda r: tuple(0 for _ in a.shape))
    hcat = pl.pallas_call(
        _filter_mlp_kernel,
        grid=(l // tr,),
        in_specs=[pl.BlockSpec((tr, 2 * emb), lambda r: (r, 0))] + [full1(a) for a in margs[1:]],
        out_specs=pl.BlockSpec((tr, 2 * hid), lambda r: (r, 0)),
        out_shape=jax.ShapeDtypeStruct((l, 2 * hid), F32),
        compiler_params=_params("parallel"),
        name="filter_mlp",
    )(*margs)
    zeros = jnp.zeros((hid, dh), F32)
    w4f = jnp.concatenate([f_w4[:, :dh], zeros], axis=0)
    w4b = jnp.concatenate([zeros, f_w4[:, dh:]], axis=0)
    max_decay = math.log(FILTER_TARGET) / FAST_DECAY
    min_decay = math.log(FILTER_TARGET) / SLOW_DECAY
    deltas = jnp.abs(jnp.linspace(min_decay, max_decay, dh, dtype=F32))[None, :]
    kern, l1 = pl.pallas_call(
        functools.partial(_filter_out_kernel, l, tr, n2),
        grid=(dh // LANES, l // tr),
        in_specs=[
            pl.BlockSpec((tr, 2 * hid), lambda c, r: (r, 0)),
            pl.BlockSpec((2 * hid, LANES), lambda c, r: (0, c)),
            pl.BlockSpec((2 * hid, LANES), lambda c, r: (0, c)),
            pl.BlockSpec((1, LANES), lambda c, r: (0, c)),
        ],
        out_specs=[pl.BlockSpec((2, _pitched_rows(tr, n2), LANES), lambda c, r: (0, r, c)),
                   pl.BlockSpec((1, LANES), lambda c, r: (0, c))],
        out_shape=[jax.ShapeDtypeStruct((2, _pitched_rows(l, n2), dh), F32), jax.ShapeDtypeStruct((1, dh), F32)],
        compiler_params=_params("parallel", "arbitrary"),
        name="filter_out",
    )(hcat, w4f, w4b, deltas)
    return kern.reshape(2 * _pitched_rows(l, n2), dh), l1


def _grid_sincos(rows, dim):
    r, col = jnp.meshgrid(jnp.arange(rows, dtype=F32), jnp.arange(GRID_W, dtype=F32), indexing='ij')
    quarter = dim // 4
    omega = 1.0 / (10000.0 ** (jnp.arange(quarter, dtype=F32) / quarter))

    def emb(pos):
        ang = pos.reshape(-1, 1) * omega[None, :]
        return jnp.concatenate([jnp.sin(ang), jnp.cos(ang)], axis=-1)

    return jnp.concatenate([emb(r), emb(col)], axis=-1)


def kernel(x, c, ctx, c_ctx, w_mod, b_mod, norm_g, w_in, hy_conv_w, hy_conv_b, f_w1, f_b1, f_w2, f_b2,
           f_w3, f_b3, f_w4, f_freq, hy_bias, rg_conv_w, rg_conv_b, rg_wa, rg_ba, rg_wx, rg_bx, rg_lam,
           br_norm_h, br_norm_r, w_out, final_g):
    B, N, D = x.shape
    L_ctx = ctx.shape[1]
    depth = w_mod.shape[0]
    dh = hy_bias.shape[1]
    dr = rg_conv_b.shape[1]
    assert B == NB and B % BB == 0 and dh == dr

    pos = _grid_sincos(N // GRID_W, D)
    cond = jnp.concatenate([c, jnp.broadcast_to(c_ctx[None], (8, D))], axis=0)
    mod = _modulation(cond, w_mod, b_mod)
    w_in_bf = w_in.astype(BF16)
    w_out_bf = w_out.astype(BF16)
    fc_lat = _fft_consts(N)
    fc_ctx = _fft_consts(L_ctx)

    n2_l, n2_c = fc_lat["n2"], fc_ctx["n2"]
    state = lambda h, rows: h[:, rows, :].transpose(1, 0, 2).reshape(NB, dr)

    xl, xc = x, ctx
    for l in range(depth):
        last = l == depth - 1
        ml = mod[l, :B].reshape(B, 1, 3 * D)
        mc = jnp.broadcast_to(mod[l, B:B + 1].reshape(1, 1, 3 * D), (B, 1, 3 * D))
        sh_l, sc_l, g_l = ml[..., :D], ml[..., D:2 * D], ml[..., 2 * D:]
        sh_c, sc_c, g_c = mc[..., :D], mc[..., D:2 * D], mc[..., 2 * D:]
        proj = (norm_g[l], w_in_bf[l], hy_conv_w[l], hy_conv_b[l], dh, dr)
        if l == 0:
            u_l, x0_l, zh_l, zr_l, xr_l, xl = _inproj(xl, pos, sh_l, sc_l, *proj, n2_l)
        else:
            u_l, x0_l, zh_l, zr_l, xr_l = _inproj(xl, None, sh_l, sc_l, *proj, n2_l)
        u_c, x0_c, zh_c, zr_c, xr_c = _inproj(xc, None, sh_c, sc_c, *proj, n2_c)
        rnn_w = _rglru_weights(rg_wa[l], rg_ba[l], rg_wx[l], rg_bx[l])
        zero = jnp.zeros((NB, dr), F32)
        hf_c, hb_c = _rglru(xr_c, rg_conv_w[l], rg_conv_b[l], *rnn_w, rg_lam[l], zero, zero)
        hf_l, hb_l = _rglru(xr_l, rg_conv_w[l], rg_conv_b[l], *rnn_w, rg_lam[l],
                            state(hf_c, slice(-NB, None)), state(hb_c, slice(0, NB)))
        filt = (f_w1[l], f_b1[l], f_w2[l], f_b2[l], f_w3[l], f_b3[l], f_w4[l], f_freq[l])
        shared = (br_norm_h[l], br_norm_r[l], w_out_bf[l])
        cv_l = _fftconv(u_l, _filter_fft(*_hyena_filter(N, n2_l, *filt), hy_bias[l], fc_lat), fc_lat)
        if not last:
            cv_c = _fftconv(u_c, _filter_fft(*_hyena_filter(L_ctx, n2_c, *filt), hy_bias[l], fc_ctx), fc_ctx)
            xc = _outproj(cv_c, x0_c, n2_c, hf_c, hb_c, zh_c, zr_c, *shared, xc, g_c, None)
        xl = _outproj(cv_l, x0_l, n2_l, hf_l, hb_l, zh_l, zr_l, *shared, xl, g_l,
                      final_g if last else None)
    return xl
```

```python
import functools
import math

import jax
import jax.numpy as jnp
import numpy as np
from jax import lax
from jax.experimental import pallas as pl
from jax.experimental.pallas import tpu as pltpu

GRID_W = 64
FILTER_TARGET = 1e-2
FAST_DECAY = 0.3
SLOW_DECAY = 1.5
RG_C = 8.0
EPS = 1e-6

LANES = 128
F32 = jnp.float32
BF16 = jnp.bfloat16
HIGHEST = lax.Precision.HIGHEST

VMEM_LIMIT_BYTES = 56 * 1024 * 1024


def _params(*sem):
    return pltpu.CompilerParams(dimension_semantics=sem, vmem_limit_bytes=VMEM_LIMIT_BYTES)


def _silu(x):
    return x * jax.nn.sigmoid(x)


def _rms(x, g):
    return x * lax.rsqrt(jnp.mean(x * x, axis=-1, keepdims=True) + EPS) * g


def _mod_kernel(s_ref, w_ref, b_ref, o_ref):
    s = _silu(s_ref[...])
    o_ref[0] = jnp.dot(s, w_ref[0], precision=HIGHEST, preferred_element_type=F32) + b_ref[0]


def _modulation(cond, w_mod, b_mod):
    depth, d, d3 = w_mod.shape
    rows = cond.shape[0]
    tn = d // 2
    return pl.pallas_call(
        _mod_kernel,
        grid=(depth, d3 // tn),
        in_specs=[
            pl.BlockSpec((rows, d), lambda l, j: (0, 0)),
            pl.BlockSpec((1, d, tn), lambda l, j: (l, 0, j)),
            pl.BlockSpec((1, 1, tn), lambda l, j: (l, 0, j)),
        ],
        out_specs=pl.BlockSpec((1, rows, tn), lambda l, j: (l, 0, j)),
        out_shape=jax.ShapeDtypeStruct((depth, rows, d3), F32),
        compiler_params=_params("parallel", "parallel"),
        name="modulation",
    )(cond, w_mod, b_mod.reshape(depth, 1, d3))


NB = 8
HALO = 8
BB = 4
GROUP = 2


def _inproj_kernel(add_pos, tm, dh, dr, n2, *refs):
    refs = list(refs)
    x_ref, xp_ref, xn_ref = refs[:3]
    refs = refs[3:]
    if add_pos:
        pos_ref, posp_ref, posn_ref = refs[:3]
        refs = refs[3:]
    (sh_ref, sc_ref, g_ref, wh_ref, wr_ref, cw_ref, cb_ref, u_ref, x0_ref, zh_ref, zr_ref, xr_ref) = refs[:12]
    j = pl.program_id(0)
    i = pl.program_id(1)
    keep_prev = (j > 0).astype(F32)
    keep_next = (j < pl.num_programs(0) - 1).astype(F32)
    seg = tm + 2 * HALO

    for b0 in range(0, BB, GROUP):
        xs = []
        for b in range(b0, b0 + GROUP):
            xb = jnp.concatenate([xp_ref[b], x_ref[b], xn_ref[b]], axis=0)
            if add_pos:
                xb = xb + jnp.concatenate([posp_ref[...], pos_ref[...], posn_ref[...]], axis=0)
                refs[12][b] = xb[HALO:HALO + tm]
            xs.append((_rms(xb, g_ref[...]) * (1.0 + sc_ref[b]) + sh_ref[b]).astype(BF16))

        xg = jnp.concatenate(xs, axis=0)
        cparts = []
        for part in range(3):
            lo = part * dh
            ph = jnp.dot(xg, wh_ref[:, lo:lo + dh], preferred_element_type=F32)
            per_b = []
            for k in range(GROUP):
                pb = ph[k * seg:(k + 1) * seg]
                full = jnp.concatenate([pb[:HALO] * keep_prev, pb[HALO:HALO + tm], pb[HALO + tm:] * keep_next], axis=0)
                c = cb_ref[:, lo:lo + dh] + cw_ref[0:1, lo:lo + dh] * full[HALO - 1:HALO - 1 + tm]
                c = c + cw_ref[1:2, lo:lo + dh] * full[HALO:HALO + tm]
                c = c + cw_ref[2:3, lo:lo + dh] * full[HALO + 1:HALO + 1 + tm]
                per_b.append(c)
            cparts.append(per_b)
        for k, b in enumerate(range(b0, b0 + GROUP)):
            x0_ref[b] = cparts[0][k].astype(x0_ref.dtype)

            def store(r, n, v, b=b):
                for s_ in range(dh // LANES):
                    u_ref[s_, b, r:r + n, :] = _lane_part(v, s_)
            _store_pitched(store, cparts[2][k] * cparts[1][k], n2)

        xn_mid = jnp.concatenate([x[HALO:HALO + tm] for x in xs], axis=0)
        pr = jnp.dot(xn_mid, wr_ref[...], preferred_element_type=F32)
        for k, b in enumerate(range(b0, b0 + GROUP)):
            pb = pr[k * tm:(k + 1) * tm]
            zh_ref[b] = pb[:, :dh].astype(zh_ref.dtype)
            zr_ref[b] = pb[:, dh + dr:].astype(zr_ref.dtype)
            for s_ in range(dr // LANES):
                xr_ref[s_, pl.ds(i * BB + b, tm, stride=NB), :] = pb[:, dh + s_ * LANES:dh + (s_ + 1) * LANES]


def _inproj(x, pos, sh, sc, g, w_bf16, conv_w, conv_b, dh, dr, n2):
    b, l, d = x.shape
    tm = min(l, 128)
    add_pos = pos is not None
    nt = l // HALO
    per = tm // HALO
    row = lambda w: pl.BlockSpec((BB, tm, w), lambda j, i: (i, j, 0))
    prev = lambda j: jnp.maximum(j * per - 1, 0)
    nxt = lambda j: jnp.minimum((j + 1) * per, nt - 1)
    vec = pl.BlockSpec((BB, 1, d), lambda j, i: (i, 0, 0))
    full = lambda a: pl.BlockSpec(a.shape, lambda j, i: tuple(0 for _ in a.shape), pipeline_mode=pl.Buffered(1))
    in_specs = [row(d),
                pl.BlockSpec((BB, HALO, d), lambda j, i: (i, prev(j), 0)),
                pl.BlockSpec((BB, HALO, d), lambda j, i: (i, nxt(j), 0))]
    args = [x, x, x]
    if add_pos:
        in_specs += [pl.BlockSpec((tm, d), lambda j, i: (j, 0)),
                     pl.BlockSpec((HALO, d), lambda j, i: (prev(j), 0)),
                     pl.BlockSpec((HALO, d), lambda j, i: (nxt(j), 0))]
        args += [pos, pos, pos]
    w_h = w_bf16[:, :3 * dh]
    w_r = w_bf16[:, 3 * dh:]
    consts = [g.reshape(1, d), w_h, w_r, conv_w, conv_b.reshape(1, 3 * dh)]
    in_specs += [vec, vec] + [full(a) for a in consts]
    args += [sh, sc] + consts
    out_specs = [pl.BlockSpec((dh // LANES, BB, _pitched_rows(tm, n2), LANES), lambda j, i: (0, i, j, 0)),
                 row(dh), row(dh), row(dr),
                 pl.BlockSpec((dr // LANES, tm * NB, LANES), lambda j, i: (0, j, 0))]
    out_shape = [jax.ShapeDtypeStruct((dh // LANES, b, _pitched_rows(l, n2), LANES), F32),
                 jax.ShapeDtypeStruct((b, l, dh), BF16), jax.ShapeDtypeStruct((b, l, dh), BF16),
                 jax.ShapeDtypeStruct((b, l, dr), BF16),
                 jax.ShapeDtypeStruct((dr // LANES, l * NB, LANES), F32)]
    if add_pos:
        out_specs.append(row(d))
        out_shape.append(jax.ShapeDtypeStruct((b, l, d), F32))
    return pl.pallas_call(
        functools.partial(_inproj_kernel, add_pos, tm, dh, dr, n2),
        grid=(l // tm, b // BB),
        in_specs=in_specs,
        out_specs=out_specs,
        out_shape=out_shape,
        compiler_params=_params("arbitrary", "arbitrary"),
        name="inproj_pos" if add_pos else "inproj",
    )(*args)


def _outproj_kernel(final, tm, n2, *refs):
    if final:
        (cv_ref, x0_ref, hf_ref, hb_ref, zh_ref, zr_ref, gnh_ref, gnr_ref, wt_ref, wb_ref,
         x_ref, gate_ref, fg_ref, o_ref) = refs
    else:
        (cv_ref, x0_ref, hf_ref, hb_ref, zh_ref, zr_ref, gnh_ref, gnr_ref, wt_ref, wb_ref,
         x_ref, gate_ref, o_ref) = refs
    i = pl.program_id(1)
    yhs, yrs = [], []
    for b in range(BB):
        cv = _lane_cat([_load_pitched(lambda r, n, b=b, s_=s_: cv_ref[s_, b, r:r + n, :], tm, n2)
                        for s_ in range(cv_ref.shape[0])])
        y_h = cv * x0_ref[b].astype(F32)
        yhs.append((_rms(y_h, gnh_ref[...]) * _silu(zh_ref[b].astype(F32))).astype(BF16))
        hs = _lane_cat([hf_ref[s_, pl.ds(i * BB + b, tm, stride=NB), :] + hb_ref[s_, pl.ds(i * BB + b, tm, stride=NB), :]
                        for s_ in range(hf_ref.shape[0])])
        yrs.append((_rms(hs, gnr_ref[...]) * _silu(zr_ref[b].astype(F32))).astype(BF16))
    y = jnp.dot(jnp.concatenate(yhs, axis=0), wt_ref[...], preferred_element_type=F32)
    y = y + jnp.dot(jnp.concatenate(yrs, axis=0), wb_ref[...], preferred_element_type=F32)
    for b in range(BB):
        out = x_ref[b] + gate_ref[b] * y[b * tm:(b + 1) * tm]
        if final:
            out = _rms(out, fg_ref[...])
        o_ref[b] = out


def _outproj(conv, x0, n2, hf, hb, zh, zr, gn_h, gn_r, w_out_bf16, x, gate, final_g):
    b, l, d = x.shape
    dh = x0.shape[-1]
    dr = zr.shape[-1]
    tm = min(l, 128)
    final = final_g is not None
    row = lambda w: pl.BlockSpec((BB, tm, w), lambda j, i: (i, j, 0))
    pitched = pl.BlockSpec((dh // LANES, BB, _pitched_rows(tm, n2), LANES), lambda j, i: (0, i, j, 0))
    slabs = pl.BlockSpec((dr // LANES, tm * NB, LANES), lambda j, i: (0, j, 0))
    full = lambda a: pl.BlockSpec(a.shape, lambda j, i: tuple(0 for _ in a.shape), pipeline_mode=pl.Buffered(1))
    consts = [gn_h.reshape(1, dh), gn_r.reshape(1, dr), w_out_bf16[:dh], w_out_bf16[dh:]]
    in_specs = [pitched, row(dh), slabs, slabs, row(dh), row(dr)] + [full(a) for a in consts] + [
        row(d), pl.BlockSpec((BB, 1, d), lambda j, i: (i, 0, 0))]
    args = [conv, x0, hf, hb, zh, zr] + consts + [x, gate]
    if final:
        fg = final_g.reshape(1, d)
        in_specs.append(full(fg))
        args.append(fg)
    return pl.pallas_call(
        functools.partial(_outproj_kernel, final, tm, n2),
        grid=(l // tm, b // BB),
        in_specs=in_specs,
        out_specs=row(d),
        out_shape=jax.ShapeDtypeStruct((b, l, d), F32),
        compiler_params=_params("parallel", "parallel"),
        name="outproj_final" if final else "outproj",
    )(*args)


RG_PREV = 2
RG_NEXT = 1
SQRT_FLOOR = 1e-30
LOG2E = math.log2(math.e)


def _softplus(x):
    return jnp.maximum(x, 0.0) + jnp.log(1.0 + jnp.exp(-jnp.abs(x)))


def _sqrt01(y):
    return y * lax.rsqrt(jnp.maximum(y, SQRT_FLOOR))


def _rglru_kernel(tc, xf_ref, xfp_ref, xfn_ref, xb_ref, xbp_ref, xbn_ref, cw_ref, cb_ref,
                  wf_ref, bf_ref, wb_ref, bb_ref, lam_ref, h0f_ref, h0b_ref,
                  hf_ref, hb_ref, af_s, bf_s, ab_s, bb_s, cf_s, cbk_s):
    j = pl.program_id(0)
    nj = pl.num_programs(0)
    rows = tc * NB
    ns = xf_ref.shape[0]
    dr = ns * LANES
    wide = lambda ref: _lane_cat([ref[s_] for s_ in range(ns)])

    @pl.when(j == 0)
    def _():
        cf_s[...] = h0f_ref[...]
        cbk_s[...] = h0b_ref[...]

    def coeffs(x_ref, xp_ref, xn_ref, first, last, w_ref, b_ref, lam, a_s, b_s):
        prev = jnp.where(first, 0.0, wide(xp_ref))
        nxt = jnp.where(last, 0.0, wide(xn_ref))
        xfull = jnp.concatenate([prev, wide(x_ref), nxt], axis=0)
        xh = cb_ref[...] + cw_ref[0:1, :] * xfull[0:rows]
        for k in range(1, RG_PREV + RG_NEXT + 1):
            xh = xh + cw_ref[k:k + 1, :] * xfull[k * NB:k * NB + rows]
        g = jnp.dot(xh.astype(BF16), w_ref[...], preferred_element_type=F32) + b_ref[...]
        t_r = jnp.tanh(g[:, :dr])
        t_i = jnp.tanh(g[:, dr:])
        c1 = (-0.5 * RG_C * LOG2E) * _softplus(-lam)
        a = jnp.exp2(c1 * t_r + c1)
        a_s[...] = a
        b_s[...] = _sqrt01(1.0 - a * a) * (t_i * xh + xh)

    coeffs(xf_ref, xfp_ref, xfn_ref, j == 0, j == nj - 1, wf_ref, bf_ref, lam_ref[0:1, :], af_s, bf_s)
    coeffs(xb_ref, xbp_ref, xbn_ref, j == nj - 1, j == 0, wb_ref, bb_ref, lam_ref[1:2, :], ab_s, bb_s)

    def step(t, carry):
        hf, hb = carry
        rf = pl.multiple_of(t * NB, NB)
        rb = pl.multiple_of((tc - 1 - t) * NB, NB)
        hf = af_s[pl.ds(rf, NB), :] * hf + bf_s[pl.ds(rf, NB), :]
        hb = ab_s[pl.ds(rb, NB), :] * hb + bb_s[pl.ds(rb, NB), :]
        for s_ in range(ns):
            hf_ref[s_, pl.ds(rf, NB), :] = _lane_part(hf, s_)
            hb_ref[s_, pl.ds(rb, NB), :] = _lane_part(hb, s_)
        return hf, hb

    hf, hb = lax.fori_loop(0, tc, step, (cf_s[...], cbk_s[...]), unroll=8)
    cf_s[...] = hf
    cbk_s[...] = hb


def _rglru(xr2, conv_w, conv_b, w_f, b_f, w_b, b_b, lam, h0_f, h0_b):
    ns, rows_total, _ = xr2.shape
    dr = ns * LANES
    l = rows_total // NB
    tc = min(l, 128)
    nj = l // tc
    rows = tc * NB
    prev_rows = RG_PREV * NB
    last_prev = rows // prev_rows
    n_next = rows_total // NB

    def blk(shape, fn):
        return pl.BlockSpec(shape, fn)

    full = lambda shape: pl.BlockSpec(shape, lambda j: tuple(0 for _ in shape))
    in_specs = [
        blk((ns, rows, LANES), lambda j: (0, j, 0)),
        blk((ns, prev_rows, LANES), lambda j: (0, jnp.maximum(j * last_prev - 1, 0), 0)),
        blk((ns, NB, LANES), lambda j: (0, jnp.minimum((j + 1) * tc, n_next - 1), 0)),
        blk((ns, rows, LANES), lambda j: (0, nj - 1 - j, 0)),
        blk((ns, prev_rows, LANES), lambda j: (0, jnp.maximum((nj - 1 - j) * last_prev - 1, 0), 0)),
        blk((ns, NB, LANES), lambda j: (0, jnp.minimum((nj - j) * tc, n_next - 1), 0)),
        full(conv_w.shape), full((1, dr)),
        full(w_f.shape), full((1, 2 * dr)), full(w_b.shape), full((1, 2 * dr)),
        full(lam.shape), full((NB, dr)), full((NB, dr)),
    ]
    out_specs = [blk((ns, rows, LANES), lambda j: (0, j, 0)), blk((ns, rows, LANES), lambda j: (0, nj - 1 - j, 0))]
    out_shape = [jax.ShapeDtypeStruct((ns, rows_total, LANES), F32)] * 2
    scratch = [pltpu.VMEM((rows, dr), F32)] * 4 + [pltpu.VMEM((NB, dr), F32)] * 2
    return pl.pallas_call(
        functools.partial(_rglru_kernel, tc),
        grid=(nj,),
        in_specs=in_specs,
        out_specs=out_specs,
        out_shape=out_shape,
        scratch_shapes=scratch,
        compiler_params=_params("arbitrary"),
        name="rglru",
    )(xr2, xr2, xr2, xr2, xr2, xr2, 0.5 * conv_w, 0.5 * conv_b.reshape(1, dr), w_f, 0.5 * b_f, w_b, 0.5 * b_b,
      lam, h0_f, h0_b)


def _block_diag(w):
    heads, blk, _ = w.shape
    eye = jnp.eye(heads, dtype=w.dtype)
    return jnp.einsum('hij,hg->higj', w, eye).reshape(heads * blk, heads * blk)


def _rglru_weights(wa, ba, wx, bx):
    out = []
    for d in range(2):
        w = jnp.concatenate([_block_diag(wa[d]), _block_diag(wx[d])], axis=1).astype(BF16)
        b = jnp.concatenate([ba[d], bx[d]])[None, :]
        out += [w, b]
    return out


TW_B = 8
ROW_PAD = 8
SLAB_PAD = 4
MIN_STREAM_BYTES = 1 << 20
FFT_COLS = 32
FFT_SLABS = 16


def _fft_dims(l):
    n = 2 * l
    n1 = 16
    while (2 * n1) * (2 * n1) <= n:
        n1 *= 2
    return n1, n // n1


def _pitched_rows(rows, n2):
    return rows // n2 * (n2 + ROW_PAD)


def _store_pitched(store, val, n2):
    p = n2 + ROW_PAD
    pad = jnp.zeros((ROW_PAD, val.shape[1]), val.dtype)
    for g in range(val.shape[0] // n2):
        store(g * p, n2, val[g * n2:(g + 1) * n2])
        store(g * p + n2, ROW_PAD, pad)


def _load_pitched(load, rows, n2):
    p = n2 + ROW_PAD
    return jnp.concatenate([load(g * p, n2) for g in range(rows // n2)], axis=0)


def _slab_pitch(n2):
    return 2 * n2 + SLAB_PAD


def _fft_consts(l):
    n1, n2 = _fft_dims(l)
    n = n1 * n2
    h = n1 // 2
    f1 = np.exp(-2j * np.pi * np.outer(np.arange(n1), np.arange(n1)) / n1)
    f2 = np.exp(-2j * np.pi * np.outer(np.arange(n2), np.arange(n2)) / n2)
    stack = lambda a: np.block([[a.real, -a.imag], [a.imag, a.real]])
    bf = lambda a: jnp.asarray(a, F32).astype(BF16)

    g, r = np.divmod(np.arange(2 * n2), 16)
    perm = np.where(r < 8, 8 * g + r, n2 + 8 * g + (r - 8))
    inter = lambda a: a[perm][:, perm]

    m1_full = stack(f1)
    m1_half = np.concatenate([m1_full[:, :h], m1_full[:, n1:n1 + h]], axis=1)
    m3_full = stack(np.conj(f1))
    m3 = np.concatenate([m3_full[:h], m3_full[n1:n1 + h]], axis=0)
    na = n1 // TW_B
    p = np.arange(n1)
    k1_of_p = TW_B * (p % na) + p // na
    order = np.concatenate([k1_of_p, n1 + k1_of_p])
    m1_half, m1_real, m3 = m1_half[order], m1_full[:, :n1][order], m3[:, order]
    ang_a = -2.0 * np.pi * np.outer(np.arange(na) * TW_B, np.arange(n2)) / n
    ang_b = -2.0 * np.pi * np.outer(np.arange(TW_B), np.arange(n2)) / n
    ta = np.concatenate([np.cos(ang_a), np.sin(ang_a)], axis=1)[:, perm]
    ta = jnp.asarray(np.broadcast_to(ta[..., None], ta.shape + (LANES,)), F32)
    tw_b = np.exp(1j * ang_b)
    fwd = [inter(stack(f2 * tw_b[b][None, :])) for b in range(TW_B)]
    inv = [inter(stack(np.conj(f2) * np.conj(tw_b[b])[:, None])) for b in range(TW_B)]
    return dict(n1=n1, n2=n2, m1=bf(m1_half), m3=bf(m3), ta=ta,
                m2b=jnp.stack([bf(m) for m in fwd]), m2ib=jnp.stack([bf(m) for m in inv]),
                m1_real=bf(m1_real))


def _mm(m, x):
    return jnp.dot(m, x.astype(BF16), preferred_element_type=F32)


def _lane_cat(parts):
    return parts[0] if len(parts) == 1 else jnp.concatenate(parts, axis=1)


def _lane_part(x, i):
    return x[:, i * LANES:(i + 1) * LANES]


def _split_ri(x):
    x4 = x.reshape(x.shape[0] // 16, 2, 8, x.shape[1])
    return x4[:, 0], x4[:, 1]


def _join_ri(re, im):
    return jnp.stack([re, im], axis=1).reshape(2 * re.shape[0] * 8, re.shape[2])


def _cmul(x, t, conj=False):
    xr, xi = _split_ri(x)
    tr, ti = _split_ri(t)
    if conj:
        return _join_ri(xr * tr + xi * ti, xi * tr - xr * ti)
    return _join_ri(xr * tr - xi * ti, xr * ti + xi * tr)


def _col_rows(c):
    return (c // 8) * 16 + c % 8


def _stage1(n_in, n1, n2, m_ref, load, a_ref):
    cols = min(FFT_COLS, n2)
    sp = _slab_pitch(n2)

    def body(i, carry):
        out = _mm(m_ref[...], _lane_cat([load(i * cols + q) for q in range(cols)]))
        for q in range(cols):
            r = _col_rows(i * cols + q)
            a_ref[pl.ds(r, n1, stride=sp), :] = _lane_part(out[:n1], q)
            a_ref[pl.ds(r + 8, n1, stride=sp), :] = _lane_part(out[n1:], q)
        return carry

    lax.fori_loop(0, n2 // cols, body, 0)


def _slab_rows(n2, na, a, b):
    return pl.multiple_of((b * na + a) * _slab_pitch(n2), SLAB_PAD)


def _fftconv_kernel(n1, n2, u_hbm, ks_hbm, m1_ref, m2b_ref, m2ib_ref, m3_ref, ta_ref, y_hbm,
                    a_ref, buf, kbuf, sem_in, sem_out, sem_k):
    h = n1 // 2
    pitch = n2 + ROW_PAD
    sp = _slab_pitch(n2)
    na = n1 // TW_B
    slabs = min(FFT_SLABS, na)
    groups = na // slabs
    cols = min(FFT_COLS, n2)
    npairs = pl.num_programs(1)
    total = pl.num_programs(0) * npairs
    t = pl.program_id(0) * npairs + pl.program_id(1)
    slot = t % 2

    def window(ref, tt):
        return ref.at[tt // npairs, pl.ds(2 * (tt % npairs), 2)]

    def in_copy(tt, s):
        return pltpu.make_async_copy(window(u_hbm, tt), buf.at[s], sem_in.at[s])

    def out_copy(tt, s):
        return pltpu.make_async_copy(buf.at[s], window(y_hbm, tt), sem_out.at[s])

    nchunks = TW_B * groups if slabs * sp * LANES * 4 >= MIN_STREAM_BYTES else 1
    chunk = (n1 * sp) // nchunks
    first_pair = pl.program_id(1) == 0

    def k_copy(i):
        rows0 = pl.multiple_of(i * chunk, 8)
        return pltpu.make_async_copy(ks_hbm.at[pl.program_id(0), pl.ds(rows0, chunk)],
                                     kbuf.at[pl.ds(rows0, chunk)], sem_k.at[i % 2])

    @pl.when(first_pair)
    def _():
        k_copy(0).start()

    @pl.when(t == 0)
    def _():
        in_copy(t, slot).start()

    in_copy(t, slot).wait()
    u_ref = buf.at[slot]

    def load_u(c):
        return jnp.concatenate([u_ref[0, pl.ds(c, h, stride=pitch), :],
                                u_ref[1, pl.ds(c, h, stride=pitch), :]], axis=0)

    _stage1(h, n1, n2, m1_ref, load_u, a_ref)

    @pl.when(t >= 1)
    def _():
        out_copy(t - 1, 1 - slot).wait()

    @pl.when(t + 1 < total)
    def _():
        in_copy(t + 1, 1 - slot).start()

    if nchunks == 1:
        @pl.when(first_pair)
        def _():
            k_copy(0).wait()

    def slab_stage(i, carry):
        b = i // groups
        a0 = (i % groups) * slabs
        if nchunks > 1:
            @pl.when(first_pair)
            def _():
                k_copy(i).wait()

            @pl.when(first_pair & (i + 1 < nchunks))
            def _():
                k_copy(i + 1).start()

        rows = [_slab_rows(n2, na, a0 + q, b) for q in range(slabs)]
        xs = [_cmul(a_ref[pl.ds(rows[q], 2 * n2), :], ta_ref[a0 + q]) for q in range(slabs)]
        f = _mm(m2b_ref[b], _lane_cat(xs))
        cs = [_cmul(_lane_part(f, q), kbuf[pl.ds(rows[q], 2 * n2), :]) for q in range(slabs)]
        d = _mm(m2ib_ref[b], _lane_cat(cs))
        for q in range(slabs):
            a_ref[pl.ds(rows[q], 2 * n2), :] = _cmul(_lane_part(d, q), ta_ref[a0 + q], conj=True)
        return carry

    lax.fori_loop(0, TW_B * groups, slab_stage, 0)

    def stage3(i, carry):
        xs = []
        for q in range(cols):
            r = _col_rows(i * cols + q)
            xs.append(jnp.concatenate([a_ref[pl.ds(r, n1, stride=sp), :],
                                       a_ref[pl.ds(r + 8, n1, stride=sp), :]], axis=0))
        out = _mm(m3_ref[...], _lane_cat(xs))
        for q in range(cols):
            c = i * cols + q
            u_ref[0, pl.ds(c, h, stride=pitch), :] = _lane_part(out[:h], q)
            u_ref[1, pl.ds(c, h, stride=pitch), :] = _lane_part(out[h:], q)
        return carry

    lax.fori_loop(0, n2 // cols, stage3, 0)

    out_copy(t, slot).start()

    @pl.when(t == total - 1)
    def _():
        out_copy(t, slot).wait()


def _const_spec(a):
    return pl.BlockSpec(a.shape, lambda *_: tuple(0 for _ in a.shape), pipeline_mode=pl.Buffered(1))


def _fftconv(u, kspec, fc):
    ns, b, lp, _ = u.shape
    dh = ns * LANES
    n1, n2 = fc["n1"], fc["n2"]
    nslab = n1 * _slab_pitch(n2)
    consts = [fc["m1"], fc["m2b"], fc["m2ib"], fc["m3"], fc["ta"]]
    return pl.pallas_call(
        functools.partial(_fftconv_kernel, n1, n2),
        grid=(dh // LANES, b // 2),
        in_specs=[
            pl.BlockSpec(memory_space=pl.ANY),
            pl.BlockSpec(memory_space=pl.ANY),
        ] + [_const_spec(a) for a in consts],
        out_specs=pl.BlockSpec(memory_space=pl.ANY),
        out_shape=jax.ShapeDtypeStruct(u.shape, F32),
        scratch_shapes=[pltpu.VMEM((nslab, LANES), F32), pltpu.VMEM((2, 2, lp, LANES), F32),
                        pltpu.VMEM((nslab, LANES), F32),
                        pltpu.SemaphoreType.DMA((2,)), pltpu.SemaphoreType.DMA((2,)), pltpu.SemaphoreType.DMA((2,))],
        compiler_params=_params("arbitrary", "arbitrary"),
        name="fftconv",
    )(u, kspec, *consts)


def _filter_fft_kernel(n1, n2, k_ref, l1_ref, hyb_ref, m1_ref, m2b_ref, ta_ref, o_ref):
    scale = (1.0 / (n1 * n2)) / l1_ref[...]
    skip = hyb_ref[...] * (1.0 / (n1 * n2))
    pitch = n2 + ROW_PAD
    na = n1 // TW_B
    slabs = min(FFT_SLABS, na)
    groups = na // slabs

    _stage1(n1, n1, n2, m1_ref, lambda c: k_ref[pl.ds(c, n1, stride=pitch), :], o_ref)

    def slab_stage(i, carry):
        b = i // groups
        a0 = (i % groups) * slabs
        rows = [_slab_rows(n2, na, a0 + q, b) for q in range(slabs)]
        xs = [_cmul(o_ref[pl.ds(rows[q], 2 * n2), :], ta_ref[a0 + q]) for q in range(slabs)]
        f = _mm(m2b_ref[b], _lane_cat(xs))
        for q in range(slabs):
            fr, fi = _split_ri(_lane_part(f, q) * scale)
            o_ref[pl.ds(rows[q], 2 * n2), :] = _join_ri(fr + skip, fi)
            o_ref[pl.ds(rows[q] + 2 * n2, SLAB_PAD), :] = jnp.zeros((SLAB_PAD, LANES), F32)
        return carry

    lax.fori_loop(0, TW_B * groups, slab_stage, 0)


def _filter_fft(kern, l1, hy_bias, fc):
    npad, dh = kern.shape
    n1, n2 = fc["n1"], fc["n2"]
    nslab = n1 * _slab_pitch(n2)
    consts = [fc["m1_real"], fc["m2b"], fc["ta"]]
    return pl.pallas_call(
        functools.partial(_filter_fft_kernel, n1, n2),
        grid=(dh // LANES,),
        in_specs=[pl.BlockSpec((npad, LANES), lambda c: (0, c), pipeline_mode=pl.Buffered(1)),
                  pl.BlockSpec((1, LANES), lambda c: (0, c)),
                  pl.BlockSpec((1, LANES), lambda c: (0, c))]
        + [_const_spec(a) for a in consts],
        out_specs=pl.BlockSpec((None, nslab, LANES), lambda c: (c, 0, 0)),
        out_shape=jax.ShapeDtypeStruct((dh // LANES, nslab, LANES), F32),
        compiler_params=_params("parallel"),
        name="filter_fft",
    )(kern, l1, hy_bias.reshape(1, dh), *consts)


def _filter_mlp_kernel(z_ref, w1_ref, b1_ref, w2_ref, b2_ref, w3_ref, b3_ref, fr_ref, h_ref):
    dot = lambda a, b: jnp.dot(a, b, precision=HIGHEST, preferred_element_type=F32)
    h = jnp.sin(fr_ref[0:1, :] * (dot(z_ref[...], w1_ref[...]) + b1_ref[...]))
    h = jnp.sin(fr_ref[1:2, :] * (dot(h, w2_ref[...]) + b2_ref[...]))
    h_ref[...] = jnp.sin(fr_ref[2:3, :] * (dot(h, w3_ref[...]) + b3_ref[...]))


def _filter_out_kernel(l, tr, n2, h_ref, w4f_ref, w4b_ref, dl_ref, o_ref, l1_ref):
    rc = pl.program_id(1)
    hcat = h_ref[...]
    hf = jnp.dot(hcat, w4f_ref[...], precision=HIGHEST, preferred_element_type=F32)
    hb = jnp.dot(hcat, w4b_ref[...], precision=HIGHEST, preferred_element_type=F32)
    row = lax.broadcasted_iota(jnp.int32, (tr, LANES), 0) + rc * tr
    step = 1.0 / (l - 1)
    t_f = row.astype(F32) * step
    t_b = (l - row).astype(F32) * step
    kf = hf * jnp.exp(-t_f * dl_ref[...])
    kb = jnp.where(row == 0, 0.0, hb * jnp.exp(-t_b * dl_ref[...]))
    for half, val in ((0, kf), (1, kb)):
        def store(r, n, v, half=half):
            o_ref[half, r:r + n, :] = v
        _store_pitched(store, val, n2)
    part = jnp.sum(jnp.abs(kf), axis=0, keepdims=True) + jnp.sum(jnp.abs(kb), axis=0, keepdims=True)

    @pl.when(rc == 0)
    def _():
        l1_ref[...] = part

    @pl.when(rc > 0)
    def _():
        l1_ref[...] += part


def _filter_feats(l, bands):
    f32 = np.float32
    t = np.linspace(0.0, 1.0, l, dtype=f32)[:, None]
    w = (f32(2.0 * math.pi) * np.arange(l, dtype=f32)[:, None] / f32(l)).astype(f32)
    f = np.linspace(1e-4, bands - 1, bands, dtype=f32)[None, :]
    z = np.concatenate([t, np.cos(f * w), -np.sin(f * w)], axis=-1).astype(f32)
    z_rev = np.concatenate([z[:1], z[1:][::-1]], axis=0)
    return jnp.asarray(np.concatenate([z, z_rev], axis=-1))


def _hyena_filter(l, n2, f_w1, f_b1, f_w2, f_b2, f_w3, f_b3, f_w4, f_freq):
    emb, hid = f_w1.shape
    dh = f_w4.shape[1] // 2
    tr = min(l, 1024)
    z2 = _filter_feats(l, (emb - 1) // 2)
    bd = lambda w: jnp.kron(jnp.eye(2, dtype=F32), w)
    two = lambda v: jnp.concatenate([v, v], axis=-1)
    margs = [z2, bd(f_w1), two(f_b1)[None], bd(f_w2), two(f_b2)[None], bd(f_w3), two(f_b3)[None], two(f_freq)]
    full1 = lambda a: pl.BlockSpec(a.shape, lambda r: tuple(0 for _ in a.shape))
    hcat = pl.pallas_call(
        _filter_mlp_kernel,
        grid=(l // tr,),
        in_specs=[pl.BlockSpec((tr, 2 * emb), lambda r: (r, 0))] + [full1(a) for a in margs[1:]],
        out_specs=pl.BlockSpec((tr, 2 * hid), lambda r: (r, 0)),
        out_shape=jax.ShapeDtypeStruct((l, 2 * hid), F32),
        compiler_params=_params("parallel"),
        name="filter_mlp",
    )(*margs)
    zeros = jnp.zeros((hid, dh), F32)
    w4f = jnp.concatenate([f_w4[:, :dh], zeros], axis=0)
    w4b = jnp.concatenate([zeros, f_w4[:, dh:]], axis=0)
    max_decay = math.log(FILTER_TARGET) / FAST_DECAY
    min_decay = math.log(FILTER_TARGET) / SLOW_DECAY
    deltas = jnp.abs(jnp.linspace(min_decay, max_decay, dh, dtype=F32))[None, :]
    kern, l1 = pl.pallas_call(
        functools.partial(_filter_out_kernel, l, tr, n2),
        grid=(dh // LANES, l // tr),
        in_specs=[
            pl.BlockSpec((tr, 2 * hid), lambda c, r: (r, 0)),
            pl.BlockSpec((2 * hid, LANES), lambda c, r: (0, c)),
            pl.BlockSpec((2 * hid, LANES), lambda c, r: (0, c)),
            pl.BlockSpec((1, LANES), lambda c, r: (0, c)),
        ],
        out_specs=[pl.BlockSpec((2, _pitched_rows(tr, n2), LANES), lambda c, r: (0, r, c)),
                   pl.BlockSpec((1, LANES), lambda c, r: (0, c))],
        out_shape=[jax.ShapeDtypeStruct((2, _pitched_rows(l, n2), dh), F32), jax.ShapeDtypeStruct((1, dh), F32)],
        compiler_params=_params("parallel", "arbitrary"),
        name="filter_out",
    )(hcat, w4f, w4b, deltas)
    return kern.reshape(2 * _pitched_rows(l, n2), dh), l1


def _grid_sincos(rows, dim):
    r, col = jnp.meshgrid(jnp.arange(rows, dtype=F32), jnp.arange(GRID_W, dtype=F32), indexing='ij')
    quarter = dim // 4
    omega = 1.0 / (10000.0 ** (jnp.arange(quarter, dtype=F32) / quarter))

    def emb(pos):
        ang = pos.reshape(-1, 1) * omega[None, :]
        return jnp.concatenate([jnp.sin(ang), jnp.cos(ang)], axis=-1)

    return jnp.concatenate([emb(r), emb(col)], axis=-1)


def kernel(x, c, ctx, c_ctx, w_mod, b_mod, norm_g, w_in, hy_conv_w, hy_conv_b, f_w1, f_b1, f_w2, f_b2,
           f_w3, f_b3, f_w4, f_freq, hy_bias, rg_conv_w, rg_conv_b, rg_wa, rg_ba, rg_wx, rg_bx, rg_lam,
           br_norm_h, br_norm_r, w_out, final_g):
    B, N, D = x.shape
    L_ctx = ctx.shape[1]
    depth = w_mod.shape[0]
    dh = hy_bias.shape[1]
    dr = rg_conv_b.shape[1]
    assert B == NB and B % BB == 0 and dh == dr

    pos = _grid_sincos(N // GRID_W, D)
    cond = jnp.concatenate([c, jnp.broadcast_to(c_ctx[None], (8, D))], axis=0)
    mod = _modulation(cond, w_mod, b_mod)
    w_in_bf = w_in.astype(BF16)
    w_out_bf = w_out.astype(BF16)
    fc_lat = _fft_consts(N)
    fc_ctx = _fft_consts(L_ctx)

    n2_l, n2_c = fc_lat["n2"], fc_ctx["n2"]
    state = lambda h, rows: h[:, rows, :].transpose(1, 0, 2).reshape(NB, dr)

    xl, xc = x, ctx
    for l in range(depth):
        last = l == depth - 1
        ml = mod[l, :B].reshape(B, 1, 3 * D)
        mc = jnp.broadcast_to(mod[l, B:B + 1].reshape(1, 1, 3 * D), (B, 1, 3 * D))
        sh_l, sc_l, g_l = ml[..., :D], ml[..., D:2 * D], ml[..., 2 * D:]
        sh_c, sc_c, g_c = mc[..., :D], mc[..., D:2 * D], mc[..., 2 * D:]
        proj = (norm_g[l], w_in_bf[l], hy_conv_w[l], hy_conv_b[l], dh, dr)
        if l == 0:
            u_l, x0_l, zh_l, zr_l, xr_l, xl = _inproj(xl, pos, sh_l, sc_l, *proj, n2_l)
        else:
            u_l, x0_l, zh_l, zr_l, xr_l = _inproj(xl, None, sh_l, sc_l, *proj, n2_l)
        u_c, x0_c, zh_c, zr_c, xr_c = _inproj(xc, None, sh_c, sc_c, *proj, n2_c)
        rnn_w = _rglru_weights(rg_wa[l], rg_ba[l], rg_wx[l], rg_bx[l])
        zero = jnp.zeros((NB, dr), F32)
        hf_c, hb_c = _rglru(xr_c, rg_conv_w[l], rg_conv_b[l], *rnn_w, rg_lam[l], zero, zero)
        hf_l, hb_l = _rglru(xr_l, rg_conv_w[l], rg_conv_b[l], *rnn_w, rg_lam[l],
                            state(hf_c, slice(-NB, None)), state(hb_c, slice(0, NB)))
        filt = (f_w1[l], f_b1[l], f_w2[l], f_b2[l], f_w3[l], f_b3[l], f_w4[l], f_freq[l])
        shared = (br_norm_h[l], br_norm_r[l], w_out_bf[l])
        cv_l = _fftconv(u_l, _filter_fft(*_hyena_filter(N, n2_l, *filt), hy_bias[l], fc_lat), fc_lat)
        if not last:
            cv_c = _fftconv(u_c, _filter_fft(*_hyena_filter(L_ctx, n2_c, *filt), hy_bias[l], fc_ctx), fc_ctx)
            xc = _outproj(cv_c, x0_c, n2_c, hf_c, hb_c, zh_c, zr_c, *shared, xc, g_c, None)
        xl = _outproj(cv_l, x0_l, n2_l, hf_l, hb_l, zh_l, zr_l, *shared, xl, g_l,
                      final_g if last else None)
    return xl
```
